```python
import jax, jax.numpy as jnp
from jax import lax
import numpy as np

D_MODEL = 1024
BATCH = 16
SEQ = 256
DEPTH = 4
DEC_BATCH = 4
DEC_SEQ = 4096
PAST_LEN = 512

GRID_W = 64
N_MIXERS = 3
N_A = (DEPTH + 2) // 3
N_B = (DEPTH + 1) // 3
N_C = DEPTH // 3
N_MOD = 9
D_FF = 2816
CHUNK = 128
D_A = 2 * D_MODEL
G_A = 8
C_A = D_A // G_A
HQ_B = 8
KV_B = 2
HD_B = D_MODEL // HQ_B
HQ_C = 16
KV_C = 2
HD_C = D_MODEL // HQ_C
WINDOW = 128
QBLK = 128
ROPE_THETA = 10000.0
EPS = 1e-6
NEG_INF = -1e30

kernel_name = 'hybrid_flow_prefix_trunk_step'


def rmsnorm(x, g):
    xf = x.astype(jnp.float32)
    y = xf * lax.rsqrt(jnp.mean(xf * xf, axis=-1, keepdims=True) + EPS)
    return (y * g.astype(jnp.float32)).astype(x.dtype)


def modulate(h, shift, scale):
    return h * (1 + scale[:, None, :]) + shift[:, None, :]


def ada_params(cond, w_mod, b_mod):
    m = jax.nn.silu(cond) @ w_mod + b_mod
    return m.reshape(cond.shape[0], N_MOD, D_MODEL)


def swiglu(h, w_gate, w_up, w_down):
    return (jax.nn.silu(h @ w_gate) * (h @ w_up)) @ w_down


def ffn_half_step(x, mod, k, g, w_gate, w_up, w_down):
    h = modulate(rmsnorm(x, g), mod[:, k], mod[:, k + 1])
    return x + 0.5 * mod[:, k + 2][:, None, :] * swiglu(h, w_gate, w_up, w_down)


def chunk_gating_mlp(h, w_in, norm_g, w_s, b_s, w_out):
    n, t, _ = h.shape
    u, v = jnp.split(h @ w_in, 2, axis=-1)
    v = rmsnorm(v, norm_g).reshape(n, t // CHUNK, CHUNK, G_A, C_A)
    v = jnp.einsum('gpq,bnqgc->bnpgc', w_s, v) + b_s.T[None, None, :, :, None]
    return (u * v.reshape(n, t, D_A)) @ w_out


def axial_rope_tables(n, hd):
    rows = n // GRID_W
    row = jnp.repeat(jnp.arange(rows), GRID_W).astype(jnp.float32)
    col = (jnp.arange(n) % GRID_W).astype(jnp.float32)
    quarter = hd // 4
    inv = ROPE_THETA ** (-jnp.arange(quarter, dtype=jnp.float32) / quarter)
    ang_r = row[:, None] * inv[None, :]
    ang_c = col[:, None] * inv[None, :]
    return jnp.cos(ang_r), jnp.sin(ang_r), jnp.cos(ang_c), jnp.sin(ang_c)


def _rotate(z, cos, sin):
    z1, z2 = jnp.split(z, 2, axis=-1)
    cos = cos[None, :, None, :]
    sin = sin[None, :, None, :]
    return jnp.concatenate([z1 * cos - z2 * sin, z2 * cos + z1 * sin], axis=-1)


def axial_rope(x, cos_r, sin_r, cos_c, sin_c):
    xf = x.astype(jnp.float32)
    half = x.shape[-1] // 2
    out = jnp.concatenate([_rotate(xf[..., :half], cos_r, sin_r), _rotate(xf[..., half:], cos_c, sin_c)], axis=-1)
    return out.astype(x.dtype)


def gqa_softmax(q, k, v, mask, sink):
    kv, g, hd = q.shape[2], q.shape[3], q.shape[4]
    s = jnp.einsum('bqkgd,bskd->bkgqs', q, k).astype(jnp.float32) * (hd ** -0.5)
    if mask is not None:
        s = jnp.where(mask, s, NEG_INF)
    if sink is None:
        p = jax.nn.softmax(s, axis=-1)
    else:
        sk = sink.astype(jnp.float32).reshape(kv, g)[None, :, :, None, None]
        m = jnp.maximum(jnp.max(s, axis=-1, keepdims=True), sk)
        e = jnp.exp(s - m)
        p = e / (jnp.sum(e, axis=-1, keepdims=True) + jnp.exp(sk - m))
    return jnp.einsum('bkgqs,bskd->bqkgd', p.astype(v.dtype), v)


def dense_blocked_attention(q, k, v, sink):
    n, t, hq, hd = q.shape
    kv = k.shape[2]
    nb = t // QBLK
    qb = q.reshape(n, nb, QBLK, kv, hq // kv, hd).swapaxes(0, 1)
    out = lax.map(lambda qi: gqa_softmax(qi, k, v, None, sink), qb)
    return out.swapaxes(0, 1).reshape(n, t, hq * hd)


def banded_attention(q, k, v, ck, cv, sink):
    n, t, hq, hd = q.shape
    kv = k.shape[2]
    g = hq // kv
    nb = t // QBLK
    sc = ck.shape[1]
    band = QBLK + 2 * WINDOW
    pad = ((0, 0), (WINDOW, WINDOW), (0, 0), (0, 0))
    kpad = jnp.pad(k, pad)
    vpad = jnp.pad(v, pad)
    ctx_mask = jnp.ones((QBLK, sc), dtype=bool)

    def block(j):
        start = j * QBLK
        qi = lax.dynamic_slice_in_dim(q, start, QBLK, axis=1).reshape(n, QBLK, kv, g, hd)
        kb = lax.dynamic_slice_in_dim(kpad, start, band, axis=1)
        vb = lax.dynamic_slice_in_dim(vpad, start, band, axis=1)
        qpos = start + jnp.arange(QBLK)
        kpos = start - WINDOW + jnp.arange(band)
        near = (jnp.abs(qpos[:, None] - kpos[None, :]) <= WINDOW) & (kpos[None, :] >= 0) & (kpos[None, :] < t)
        mask = jnp.concatenate([ctx_mask, near], axis=1)
        return gqa_softmax(qi, jnp.concatenate([ck, kb], axis=1), jnp.concatenate([cv, vb], axis=1), mask, sink)

    out = lax.map(block, jnp.arange(nb))
    return out.swapaxes(0, 1).reshape(n, t, hq * hd)


def split_qkv(h, w_qkv, hq, kv, hd):
    n, t, _ = h.shape
    q, k, v = jnp.split(h @ w_qkv, [hq * hd, (hq + kv) * hd], axis=-1)
    return q.reshape(n, t, hq, hd), k.reshape(n, t, kv, hd), v.reshape(n, t, kv, hd)


def full_attn_context(h, w_qkv, q_g, k_g, w_out):
    q, k, v = split_qkv(h, w_qkv, HQ_B, KV_B, HD_B)
    q = rmsnorm(q, q_g)
    k = rmsnorm(k, k_g)
    return dense_blocked_attention(q, k, v, None) @ w_out, k, v


def full_attn_latent(h, ck, cv, w_qkv, q_g, k_g, w_out):
    q, k, v = split_qkv(h, w_qkv, HQ_B, KV_B, HD_B)
    tables = axial_rope_tables(h.shape[1], HD_B)
    q = axial_rope(rmsnorm(q, q_g), *tables)
    k = axial_rope(rmsnorm(k, k_g), *tables)
    o = dense_blocked_attention(q, jnp.concatenate([ck, k], axis=1), jnp.concatenate([cv, v], axis=1), None)
    return o @ w_out


def window_attn_context(h, w_qkv, sink, w_out):
    q, k, v = split_qkv(h, w_qkv, HQ_C, KV_C, HD_C)
    return dense_blocked_attention(q, k, v, sink) @ w_out, k, v


def window_attn_latent(h, ck, cv, w_qkv, sink, w_out):
    q, k, v = split_qkv(h, w_qkv, HQ_C, KV_C, HD_C)
    tables = axial_rope_tables(h.shape[1], HD_C)
    q = axial_rope(q, *tables)
    k = axial_rope(k, *tables)
    return banded_attention(q, k, v, ck, cv, sink) @ w_out


def setup_inputs(seed: int = 0) -> dict:
    key = jax.random.key(seed)
    ks = jax.random.split(key, 32)

    def nrm(k, shape, scale=1.0):
        return scale * jax.random.normal(k, shape, jnp.float32)

    qkv_b = (HQ_B + 2 * KV_B) * HD_B
    qkv_c = (HQ_C + 2 * KV_C) * HD_C
    return {
        'x_prompt': nrm(ks[0], (BATCH, SEQ, D_MODEL)),
        'x_sample': nrm(ks[1], (DEC_BATCH, DEC_SEQ, D_MODEL)),
        'cache_b_k': nrm(ks[2], (DEC_BATCH, N_B, PAST_LEN, KV_B, HD_B)),
        'cache_b_v': nrm(ks[3], (DEC_BATCH, N_B, PAST_LEN, KV_B, HD_B)),
        'cache_c_k': nrm(ks[4], (DEC_BATCH, N_C, PAST_LEN, KV_C, HD_C)),
        'cache_c_v': nrm(ks[5], (DEC_BATCH, N_C, PAST_LEN, KV_C, HD_C)),
        'c': nrm(ks[6], (DEC_BATCH, D_MODEL)),
        'c_ctx': nrm(ks[7], (D_MODEL,)),
        'w_mod': nrm(ks[8], (DEPTH, D_MODEL, N_MOD * D_MODEL), 0.5 * D_MODEL ** -0.5),
        'b_mod': nrm(ks[9], (DEPTH, N_MOD * D_MODEL), 0.01),
        'norm_g': 1.0 + nrm(ks[10], (DEPTH, 3, D_MODEL), 0.02),
        'ffn_w_gate': nrm(ks[11], (DEPTH, 2, D_MODEL, D_FF), D_MODEL ** -0.5),
        'ffn_w_up': nrm(ks[12], (DEPTH, 2, D_MODEL, D_FF), D_MODEL ** -0.5),
        'ffn_w_down': nrm(ks[13], (DEPTH, 2, D_FF, D_MODEL), D_FF ** -0.5),
        'a_w_in': nrm(ks[14], (N_A, D_MODEL, 2 * D_A), D_MODEL ** -0.5),
        'a_norm_g': 1.0 + nrm(ks[15], (N_A, D_A), 0.02),
        'a_w_s': nrm(ks[16], (N_A, G_A, CHUNK, CHUNK), CHUNK ** -0.5),
        'a_b_s': 1.0 + nrm(ks[17], (N_A, G_A, CHUNK), 0.1),
        'a_w_out': nrm(ks[18], (N_A, D_A, D_MODEL), D_A ** -0.5),
        'b_w_qkv': nrm(ks[19], (N_B, D_MODEL, qkv_b), D_MODEL ** -0.5),
        'b_q_g': 1.0 + nrm(ks[20], (N_B, HD_B), 0.02),
        'b_k_g': 1.0 + nrm(ks[21], (N_B, HD_B), 0.02),
        'b_w_out': nrm(ks[22], (N_B, HQ_B * HD_B, D_MODEL), (HQ_B * HD_B) ** -0.5),
        'c_w_qkv': nrm(ks[23], (N_C, D_MODEL, qkv_c), D_MODEL ** -0.5),
        'c_sink': nrm(ks[24], (N_C, HQ_C), 0.5),
        'c_w_out': nrm(ks[25], (N_C, HQ_C * HD_C, D_MODEL), (HQ_C * HD_C) ** -0.5),
        'final_g': 1.0 + nrm(ks[26], (D_MODEL,), 0.02),
    }


def reference(x_prompt, x_sample, cache_b_k, cache_b_v, cache_c_k, cache_c_v, c, c_ctx,
              w_mod, b_mod, norm_g, ffn_w_gate, ffn_w_up, ffn_w_down,
              a_w_in, a_norm_g, a_w_s, a_b_s, a_w_out,
              b_w_qkv, b_q_g, b_k_g, b_w_out,
              c_w_qkv, c_sink, c_w_out, final_g):
    xp, xs = x_prompt, x_sample
    nbk, nbv, nck, ncv = [], [], [], []
    for l in range(DEPTH):
        kind, idx = l % N_MIXERS, l // N_MIXERS
        mp = ada_params(c_ctx[None, :], w_mod[l], b_mod[l])
        ms = ada_params(c, w_mod[l], b_mod[l])
        f0 = (ffn_w_gate[l, 0], ffn_w_up[l, 0], ffn_w_down[l, 0])
        xp = ffn_half_step(xp, mp, 0, norm_g[l, 0], *f0)
        xs = ffn_half_step(xs, ms, 0, norm_g[l, 0], *f0)
        hp = modulate(rmsnorm(xp, norm_g[l, 1]), mp[:, 3], mp[:, 4])
        hs = modulate(rmsnorm(xs, norm_g[l, 1]), ms[:, 3], ms[:, 4])
        if kind == 0:
            a = (a_w_in[idx], a_norm_g[idx], a_w_s[idx], a_b_s[idx], a_w_out[idx])
            op = chunk_gating_mlp(hp, *a)
            os_ = chunk_gating_mlp(hs, *a)
        elif kind == 1:
            op, kp, vp = full_attn_context(hp, b_w_qkv[idx], b_q_g[idx], b_k_g[idx], b_w_out[idx])
            nbk.append(kp)
            nbv.append(vp)
            os_ = full_attn_latent(hs, cache_b_k[:, idx], cache_b_v[:, idx], b_w_qkv[idx], b_q_g[idx], b_k_g[idx], b_w_out[idx])
        else:
            op, kp, vp = window_attn_context(hp, c_w_qkv[idx], c_sink[idx], c_w_out[idx])
            nck.append(kp)
            ncv.append(vp)
            os_ = window_attn_latent(hs, cache_c_k[:, idx], cache_c_v[:, idx], c_w_qkv[idx], c_sink[idx], c_w_out[idx])
        xp = xp + mp[:, 5][:, None, :] * op
        xs = xs + ms[:, 5][:, None, :] * os_
        f1 = (ffn_w_gate[l, 1], ffn_w_up[l, 1], ffn_w_down[l, 1])
        xp = ffn_half_step(xp, mp, 6, norm_g[l, 2], *f1)
        xs = ffn_half_step(xs, ms, 6, norm_g[l, 2], *f1)
    y_prompt = rmsnorm(xp, final_g)
    y_sample = rmsnorm(xs, final_g)
    new_b_k = jnp.stack(nbk, axis=1)
    new_b_v = jnp.stack(nbv, axis=1)
    new_c_k = jnp.stack(nck, axis=1)
    new_c_v = jnp.stack(ncv, axis=1)
    return (y_prompt, y_sample, new_b_k, new_b_v, new_c_k, new_c_v)
```

```python
import functools

import jax
import jax.numpy as jnp
from jax import lax
from jax.experimental import pallas as pl
from jax.experimental.pallas import tpu as pltpu

F32 = jnp.float32
BF16 = jnp.bfloat16

D_MODEL = 1024
BATCH = 16
SEQ = 256
DEPTH = 4
DEC_BATCH = 4
DEC_SEQ = 4096
PAST_LEN = 512
GRID_W = 64
N_MOD = 9
D_FF = 2816
CHUNK = 128
D_A = 2 * D_MODEL
G_A = 8
C_A = D_A // G_A
HQ_B, KV_B, HD_B = 8, 2, 128
HQ_C, KV_C, HD_C = 16, 2, 64
WINDOW = 128
ROPE_THETA = 10000.0
EPS = 1e-6
NEG_INF = -1e30

N_CTX = BATCH * SEQ
N_LAT = DEC_BATCH * DEC_SEQ
N_TOK = N_CTX + N_LAT
GROUP_TOK = 4096
N_GROUPS = N_TOK // GROUP_TOK
MOD_ROWS = 8
LANES = 128

TM = 512
TQ_LAT = 128
BAND = 3 * WINDOW
VMEM_LIMIT = 56 * 1024 * 1024


def _params(n_grid):
    return pltpu.CompilerParams(dimension_semantics=("arbitrary",) * n_grid,
                                vmem_limit_bytes=VMEM_LIMIT)


def _resident(block_shape, index_map):
    return pl.BlockSpec(block_shape, index_map, pipeline_mode=pl.Buffered(1))


def _dot(a, b):
    return jnp.dot(a, b, preferred_element_type=F32)


def _dot_nt(a, b):
    return lax.dot_general(a, b, (((1,), (1,)), ((), ())), preferred_element_type=F32)


def _rms(x):
    return x * lax.rsqrt(jnp.mean(x * x, axis=-1, keepdims=True) + EPS)


def _norm_mod(x, g, shift, scale):
    return (_rms(x) * g) * (1.0 + scale) + shift


ADA_TN = 2304


def _ada_kernel(cond_ref, w_ref, b_ref, o_ref):
    s = jax.nn.silu(cond_ref[...]).astype(BF16)
    o_ref[...] = _dot(s, w_ref[...].astype(BF16)) + b_ref[...]


def _ada_params(cond, w_mod, b_mod):
    n = N_MOD * D_MODEL
    out = pl.pallas_call(
        _ada_kernel,
        grid=(DEPTH, n // ADA_TN),
        in_specs=[
            pl.BlockSpec((MOD_ROWS, D_MODEL), lambda l, j: (0, 0)),
            pl.BlockSpec((None, D_MODEL, ADA_TN), lambda l, j: (l, 0, j)),
            pl.BlockSpec((None, 1, ADA_TN), lambda l, j: (l, 0, j)),
        ],
        out_specs=pl.BlockSpec((None, MOD_ROWS, ADA_TN), lambda l, j: (l, 0, j)),
        out_shape=jax.ShapeDtypeStruct((DEPTH, MOD_ROWS, n), F32),
        compiler_params=_params(2),
        name="ada_params",
    )(cond, w_mod, b_mod.reshape(DEPTH, 1, n))
    return out.reshape(DEPTH, MOD_ROWS, N_MOD, D_MODEL)


def _ffn_kernel(x_ref, mod_ref, g_ref, wg_ref, wu_ref, wd_ref, *rest, k, gi, final):
    x = x_ref[...]
    h = _norm_mod(x, g_ref[gi:gi + 1, :], mod_ref[k:k + 1, :], mod_ref[k + 1:k + 2, :]).astype(BF16)
    a = (jax.nn.silu(_dot(h, wg_ref[...])) * _dot(h, wu_ref[...])).astype(BF16)
    y = x + (0.5 * mod_ref[k + 2:k + 3, :]) * _dot(a, wd_ref[...])
    if final:
        fg_ref, o_ref = rest
        y = _rms(y) * fg_ref[...]
    else:
        (o_ref,) = rest
    o_ref[...] = y


def _ffn(x, mods, norm_g, wg, wu, wd, l, half, *, final_g=None, rows=None):
    k, gi = (0, 0) if half == 0 else (6, 2)
    r0, r1 = (0, N_TOK) if rows is None else rows
    t0, nt = r0 // TM, (r1 - r0) // TM
    per_group = GROUP_TOK // TM
    final = final_g is not None
    in_specs = [
        pl.BlockSpec((TM, D_MODEL), lambda i: (t0 + i, 0)),
        pl.BlockSpec((None, None, N_MOD, D_MODEL), lambda i: (l, (t0 + i) // per_group, 0, 0)),
        pl.BlockSpec((None, 3, D_MODEL), lambda i: (l, 0, 0)),
        _resident((None, None, D_MODEL, D_FF), lambda i: (l, half, 0, 0)),
        _resident((None, None, D_MODEL, D_FF), lambda i: (l, half, 0, 0)),
        _resident((None, None, D_FF, D_MODEL), lambda i: (l, half, 0, 0)),
    ]
    args = [x, mods, norm_g, wg, wu, wd]
    if final:
        in_specs.append(pl.BlockSpec((1, D_MODEL), lambda i: (0, 0)))
        args.append(final_g.reshape(1, D_MODEL))
    return pl.pallas_call(
        functools.partial(_ffn_kernel, k=k, gi=gi, final=final),
        grid=(nt,),
        in_specs=in_specs,
        out_specs=pl.BlockSpec((TM, D_MODEL), lambda i: (i, 0)),
        out_shape=jax.ShapeDtypeStruct((nt * TM, D_MODEL), F32),
        compiler_params=_params(1),
        name=f"ffn_l{l}_h{half}" + ("_final" if final else ""),
    )(*args)


def _sgu_kernel(x_ref, mod_ref, g_ref, win_ref, ng_ref, ws_ref, bs_ref, wout_ref, o_ref, gated_ref):
    x = x_ref[...]
    h = _norm_mod(x, g_ref[1:2, :], mod_ref[3:4, :], mod_ref[4:5, :]).astype(BF16)
    uv = _dot(h, win_ref[...])
    v = (_rms(uv[:, D_A:]) * ng_ref[...]).astype(BF16)
    for c in range(TM // CHUNK):
        rows = slice(c * CHUNK, (c + 1) * CHUNK)
        for g in range(G_A):
            cols = slice(g * C_A, (g + 1) * C_A)
            mixed = _dot(ws_ref[g], v[rows, cols]) + bs_ref[:, g:g + 1]
            gated_ref[rows, cols] = (uv[rows, cols] * mixed).astype(BF16)
    o_ref[...] = x + mod_ref[5:6, :] * _dot(gated_ref[...], wout_ref[...])


def _mixer_a(x, mods, norm_g, w_in, a_norm_g, w_s, b_s_t, w_out, l, idx):
    per_group = GROUP_TOK // TM
    return pl.pallas_call(
        _sgu_kernel,
        grid=(N_TOK // TM,),
        in_specs=[
            pl.BlockSpec((TM, D_MODEL), lambda i: (i, 0)),
            pl.BlockSpec((None, None, N_MOD, D_MODEL), lambda i: (l, i // per_group, 0, 0)),
            pl.BlockSpec((None, 3, D_MODEL), lambda i: (l, 0, 0)),
            _resident((None, D_MODEL, 2 * D_A), lambda i: (idx, 0, 0)),
            pl.BlockSpec((None, 1, D_A), lambda i: (idx, 0, 0)),
            pl.BlockSpec((None, G_A, CHUNK, CHUNK), lambda i: (idx, 0, 0, 0)),
            pl.BlockSpec((None, CHUNK, G_A), lambda i: (idx, 0, 0)),
            _resident((None, D_A, D_MODEL), lambda i: (idx, 0, 0)),
        ],
        out_specs=pl.BlockSpec((TM, D_MODEL), lambda i: (i, 0)),
        out_shape=jax.ShapeDtypeStruct((N_TOK, D_MODEL), F32),
        scratch_shapes=[pltpu.VMEM((TM, D_A), BF16)],
        compiler_params=_params(1),
        name=f"mixer_a_l{l}",
    )(x, mods, norm_g, w_in, a_norm_g, w_s, b_s_t, w_out)


def _swap_halves(x, half):
    lane = lax.broadcasted_iota(jnp.int32, x.shape, 1)
    first = (lane & (2 * half - 1)) < half
    return jnp.where(first, pltpu.roll(x, LANES - half, 1), pltpu.roll(x, half, 1))


def _rope(x, cos, sin, quarter):
    return x * cos + _swap_halves(x, quarter) * sin


def _rope_tables(hd):
    rows = DEC_SEQ // GRID_W
    row = jnp.repeat(jnp.arange(rows), GRID_W).astype(F32)
    col = (jnp.arange(DEC_SEQ) % GRID_W).astype(F32)
    quarter = hd // 4
    inv = ROPE_THETA ** (-jnp.arange(quarter, dtype=F32) / quarter)
    ang_r = row[:, None] * inv[None, :]
    ang_c = col[:, None] * inv[None, :]
    cos = jnp.concatenate([jnp.cos(ang_r)] * 2 + [jnp.cos(ang_c)] * 2, axis=-1)
    sin = jnp.concatenate([-jnp.sin(ang_r), jnp.sin(ang_r), -jnp.sin(ang_c), jnp.sin(ang_c)], axis=-1)
    reps = LANES // hd
    return jnp.tile(cos, (1, reps)), jnp.tile(sin, (1, reps))


QKV_B = (HQ_B + 2 * KV_B) * HD_B
QKV_C = (HQ_C + 2 * KV_C) * HD_C


def _qkv_b_kernel(x_ref, mod_ref, g_ref, w_ref, qg_ref, kg_ref, *rest, latent):
    if latent:
        cos_ref, sin_ref, q_ref, k_ref, v_ref = rest
        cos, sin = cos_ref[...], sin_ref[...]
    else:
        q_ref, k_ref, v_ref, kf_ref, vf_ref = rest
    h = _norm_mod(x_ref[...], g_ref[1:2, :], mod_ref[3:4, :], mod_ref[4:5, :]).astype(BF16)
    qkv = _dot(h, w_ref[...])
    scale = HD_B ** -0.5
    for j in range(HQ_B + KV_B):
        cols = slice(j * HD_B, (j + 1) * HD_B)
        is_q = j < HQ_B
        y = _rms(qkv[:, cols]) * (qg_ref[...] if is_q else kg_ref[...])
        if not latent and not is_q:
            kf_ref[:, (j - HQ_B) * HD_B:(j - HQ_B + 1) * HD_B] = y
        if latent:
            y = _rope(y, cos, sin, HD_B // 4)
        if is_q:
            q_ref[:, cols] = (y * scale).astype(BF16)
        else:
            k_ref[:, (j - HQ_B) * HD_B:(j - HQ_B + 1) * HD_B] = y.astype(BF16)
    v = qkv[:, (HQ_B + KV_B) * HD_B:]
    v_ref[...] = v.astype(BF16)
    if not latent:
        vf_ref[...] = v


def _qkv_b(x, mods, norm_g, w_qkv, q_g, k_g, tables, l, idx, latent):
    t0 = N_CTX // TM if latent else 0
    nt = (N_LAT if latent else N_CTX) // TM
    per_group = GROUP_TOK // TM
    kvw = KV_B * HD_B
    in_specs = [
        pl.BlockSpec((TM, D_MODEL), lambda i: (t0 + i, 0)),
        pl.BlockSpec((None, None, N_MOD, D_MODEL), lambda i: (l, (t0 + i) // per_group, 0, 0)),
        pl.BlockSpec((None, 3, D_MODEL), lambda i: (l, 0, 0)),
        _resident((None, D_MODEL, QKV_B), lambda i: (idx, 0, 0)),
        pl.BlockSpec((None, 1, HD_B), lambda i: (idx, 0, 0)),
        pl.BlockSpec((None, 1, HD_B), lambda i: (idx, 0, 0)),
    ]
    args = [x, mods, norm_g, w_qkv, q_g, k_g]
    out_specs = [
        pl.BlockSpec((TM, D_MODEL), lambda i: (i, 0)),
        pl.BlockSpec((TM, kvw), lambda i: (i, 0)),
        pl.BlockSpec((TM, kvw), lambda i: (i, 0)),
    ]
    n = nt * TM
    out_shape = [jax.ShapeDtypeStruct((n, D_MODEL), BF16), jax.ShapeDtypeStruct((n, kvw), BF16),
                 jax.ShapeDtypeStruct((n, kvw), BF16)]
    if latent:
        pos_blocks = DEC_SEQ // TM
        in_specs += [pl.BlockSpec((TM, LANES), lambda i: (i % pos_blocks, 0))] * 2
        args += list(tables)
    else:
        out_specs += [pl.BlockSpec((TM, kvw), lambda i: (i, 0))] * 2
        out_shape += [jax.ShapeDtypeStruct((n, kvw), F32)] * 2
    return pl.pallas_call(
        functools.partial(_qkv_b_kernel, latent=latent),
        grid=(nt,),
        in_specs=in_specs,
        out_specs=out_specs,
        out_shape=out_shape,
        compiler_params=_params(1),
        name=f"qkv_b_{'lat' if latent else 'ctx'}",
    )(*args)


def _attn_b_kernel(q_ref, k_ref, v_ref, *rest, tq, cached):
    if cached:
        kc_ref, vc_ref, x_ref, mod_ref, wout_ref, o_ref = rest
    else:
        x_ref, mod_ref, wout_ref, o_ref = rest
    group = HQ_B // KV_B
    heads = [None] * HQ_B
    for kh in range(KV_B):
        cols = slice(kh * HD_B, (kh + 1) * HD_B)
        q = jnp.concatenate([q_ref[:, (kh * group + g) * HD_B:(kh * group + g + 1) * HD_B]
                             for g in range(group)], axis=0)
        s = _dot_nt(q, k_ref[:, cols])
        m = jnp.max(s, axis=-1, keepdims=True)
        if cached:
            sc = _dot_nt(q, kc_ref[:, cols])
            m = jnp.maximum(m, jnp.max(sc, axis=-1, keepdims=True))
        e = jnp.exp(s - m)
        den = jnp.sum(e, axis=-1, keepdims=True)
        o = _dot(e.astype(BF16), v_ref[:, cols])
        if cached:
            ec = jnp.exp(sc - m)
            den = den + jnp.sum(ec, axis=-1, keepdims=True)
            o = o + _dot(ec.astype(BF16), vc_ref[:, cols])
        o = o / den
        for g in range(group):
            heads[kh * group + g] = o[g * tq:(g + 1) * tq]
    attn = jnp.concatenate(heads, axis=1).astype(BF16)
    o_ref[...] = x_ref[...] + mod_ref[5:6, :] * _dot(attn, wout_ref[...])


def _attn_b(x, q, k, v, cache, mods, w_out, l, idx, latent):
    kvw = KV_B * HD_B
    if latent:
        tq, nb, nq, skv = TQ_LAT, DEC_BATCH, DEC_SEQ // TQ_LAT, DEC_SEQ
        x0 = N_CTX // tq
    else:
        tq, nb, nq, skv = SEQ, BATCH, 1, SEQ
        x0 = 0
    per_group = GROUP_TOK // tq
    in_specs = [
        pl.BlockSpec((tq, D_MODEL), lambda b, i: (b * nq + i, 0)),
        pl.BlockSpec((skv, kvw), lambda b, i: (b, 0)),
        pl.BlockSpec((skv, kvw), lambda b, i: (b, 0)),
    ]
    args = [q, k, v]
    if latent:
        in_specs += [pl.BlockSpec((None, PAST_LEN, kvw), lambda b, i: (b, 0, 0))] * 2
        args += list(cache)
    x_spec = pl.BlockSpec((tq, D_MODEL), lambda b, i: (x0 + b * nq + i, 0))
    in_specs += [
        x_spec,
        pl.BlockSpec((None, None, N_MOD, D_MODEL), lambda b, i: (l, (x0 + b * nq + i) // per_group, 0, 0)),
        _resident((None, HQ_B * HD_B, D_MODEL), lambda b, i: (idx, 0, 0)),
    ]
    args += [x, mods, w_out]
    return pl.pallas_call(
        functools.partial(_attn_b_kernel, tq=tq, cached=latent),
        grid=(nb, nq),
        in_specs=in_specs,
        out_specs=x_spec,
        out_shape=jax.ShapeDtypeStruct((N_TOK, D_MODEL), F32),
        input_output_aliases={len(args) - 3: 0},
        compiler_params=_params(2),
        name=f"attn_b_{'lat' if latent else 'ctx'}",
    )(*args)


KV_PAD_C = 4 * LANES


def _pad_variants(y):
    lane = lax.broadcasted_iota(jnp.int32, y.shape, 1)
    low = lane < HD_C
    swapped = pltpu.roll(y, HD_C, 1)
    zero = jnp.zeros_like(y)
    return jnp.concatenate([jnp.where(low, y, zero), jnp.where(low, zero, swapped),
                            jnp.where(low, swapped, zero), jnp.where(low, zero, y)], axis=1)


def _qkv_c_kernel(x_ref, mod_ref, g_ref, w_ref, *rest, latent):
    if latent:
        cos_ref, sin_ref, q_ref, k_ref, v_ref = rest
        cos, sin = cos_ref[...], sin_ref[...]
    else:
        q_ref, k_ref, v_ref, kf_ref, vf_ref = rest
    h = _norm_mod(x_ref[...], g_ref[1:2, :], mod_ref[3:4, :], mod_ref[4:5, :]).astype(BF16)
    qkv = _dot(h, w_ref[...])
    scale = HD_C ** -0.5
    for j in range(D_MODEL // LANES):
        cols = slice(j * LANES, (j + 1) * LANES)
        y = qkv[:, cols]
        if latent:
            y = _rope(y, cos, sin, HD_C // 4)
        q_ref[:, cols] = (y * scale).astype(BF16)
    k = qkv[:, D_MODEL:D_MODEL + LANES]
    v = qkv[:, D_MODEL + LANES:]
    if latent:
        k = _rope(k, cos, sin, HD_C // 4)
    else:
        kf_ref[...] = k
        vf_ref[...] = v
    k_ref[...] = _pad_variants(k).astype(BF16)
    v_ref[...] = _pad_variants(v).astype(BF16)


def _qkv_c(x, mods, norm_g, w_qkv, tables, l, idx, latent):
    t0 = N_CTX // TM if latent else 0
    nt = (N_LAT if latent else N_CTX) // TM
    per_group = GROUP_TOK // TM
    in_specs = [
        pl.BlockSpec((TM, D_MODEL), lambda i: (t0 + i, 0)),
        pl.BlockSpec((None, None, N_MOD, D_MODEL), lambda i: (l, (t0 + i) // per_group, 0, 0)),
        pl.BlockSpec((None, 3, D_MODEL), lambda i: (l, 0, 0)),
        _resident((None, D_MODEL, QKV_C), lambda i: (idx, 0, 0)),
    ]
    args = [x, mods, norm_g, w_qkv]
    out_specs = [
        pl.BlockSpec((TM, D_MODEL), lambda i: (i, 0)),
        pl.BlockSpec((TM, KV_PAD_C), lambda i: (i, 0)),
        pl.BlockSpec((TM, KV_PAD_C), lambda i: (i, 0)),
    ]
    n = nt * TM
    out_shape = [jax.ShapeDtypeStruct((n, D_MODEL), BF16), jax.ShapeDtypeStruct((n, KV_PAD_C), BF16),
                 jax.ShapeDtypeStruct((n, KV_PAD_C), BF16)]
    if latent:
        pos_blocks = DEC_SEQ // TM
        in_specs += [pl.BlockSpec((TM, LANES), lambda i: (i % pos_blocks, 0))] * 2
        args += list(tables)
    else:
        out_specs += [pl.BlockSpec((TM, LANES), lambda i: (i, 0))] * 2
        out_shape += [jax.ShapeDtypeStruct((n, LANES), F32)] * 2
    return pl.pallas_call(
        functools.partial(_qkv_c_kernel, latent=latent),
        grid=(nt,),
        in_specs=in_specs,
        out_specs=out_specs,
        out_shape=out_shape,
        compiler_params=_params(1),
        name=f"qkv_c_{'lat' if latent else 'ctx'}",
    )(*args)


def _pad_variants_host(c):
    h0, h1 = c[:, :, 0, :], c[:, :, 1, :]
    z = jnp.zeros_like(h0)
    return jnp.concatenate([h0, z, z, h0, h1, z, z, h1], axis=-1).astype(BF16)


def _attn_c_kernel(sink_ref, q_ref, k_ref, v_ref, *rest, tq, latent):
    if latent:
        kc_ref, vc_ref, x_ref, mod_ref, wout_ref, o_ref = rest
        i = pl.program_id(1)
        start = pl.multiple_of(jnp.clip(i * tq - WINDOW, 0, DEC_SEQ - BAND), WINDOW)
        rows = pl.ds(start, BAND)
        qpos = i * tq + (lax.broadcasted_iota(jnp.int32, (tq, BAND), 0))
        kpos = start + lax.broadcasted_iota(jnp.int32, (tq, BAND), 1)
        near = jnp.abs(qpos - kpos) <= WINDOW
    else:
        x_ref, mod_ref, wout_ref, o_ref = rest
        rows = slice(None)
    group = HQ_C // KV_C
    pairs = group // 2
    if latent:
        near = jnp.concatenate([near] * pairs, axis=0)
    slabs = [None] * (KV_C * pairs)
    for kh in range(KV_C):
        q = jnp.concatenate([q_ref[:, (kh * pairs + j) * LANES:(kh * pairs + j + 1) * LANES]
                             for j in range(pairs)], axis=0)
        o = None
        for par in range(2):
            cols = slice((2 * kh + par) * LANES, (2 * kh + par + 1) * LANES)
            sink = jnp.concatenate([jnp.full((tq, 1), sink_ref[kh * group + 2 * j + par], F32)
                                    for j in range(pairs)], axis=0)
            s = _dot_nt(q, k_ref[rows, cols])
            if latent:
                s = jnp.where(near, s, NEG_INF)
                sc = _dot_nt(q, kc_ref[:, cols])
                m = jnp.maximum(jnp.max(s, axis=-1, keepdims=True), jnp.max(sc, axis=-1, keepdims=True))
            else:
                m = jnp.max(s, axis=-1, keepdims=True)
            m = jnp.maximum(m, sink)
            e = jnp.exp(s - m)
            den = jnp.sum(e, axis=-1, keepdims=True) + jnp.exp(sink - m)
            op = _dot(e.astype(BF16), v_ref[rows, cols])
            if latent:
                ec = jnp.exp(sc - m)
                den = den + jnp.sum(ec, axis=-1, keepdims=True)
                op = op + _dot(ec.astype(BF16), vc_ref[:, cols])
            op = op / den
            o = op if o is None else o + op
        for j in range(pairs):
            slabs[kh * pairs + j] = o[j * tq:(j + 1) * tq]
    attn = jnp.concatenate(slabs, axis=1).astype(BF16)
    o_ref[...] = x_ref[...] + mod_ref[5:6, :] * _dot(attn, wout_ref[...])


def _attn_c(x, q, k, v, cache, sink, mods, w_out, l, idx, latent):
    if latent:
        tq, nb, nq, skv = TQ_LAT, DEC_BATCH, DEC_SEQ // TQ_LAT, DEC_SEQ
        x0 = N_CTX // tq
    else:
        tq, nb, nq, skv = SEQ, BATCH, 1, SEQ
        x0 = 0
    per_group = GROUP_TOK // tq
    in_specs = [
        pl.BlockSpec(memory_space=pltpu.SMEM),
        pl.BlockSpec((tq, D_MODEL), lambda b, i: (b * nq + i, 0)),
        pl.BlockSpec((skv, KV_PAD_C), lambda b, i: (b, 0)),
        pl.BlockSpec((skv, KV_PAD_C), lambda b, i: (b, 0)),
    ]
    args = [sink, q, k, v]
    if latent:
        in_specs += [pl.BlockSpec((None, PAST_LEN, KV_PAD_C), lambda b, i: (b, 0, 0))] * 2
        args += list(cache)
    x_spec = pl.BlockSpec((tq, D_MODEL), lambda b, i: (x0 + b * nq + i, 0))
    in_specs += [
        x_spec,
        pl.BlockSpec((None, None, N_MOD, D_MODEL), lambda b, i: (l, (x0 + b * nq + i) // per_group, 0, 0)),
        _resident((None, HQ_C * HD_C, D_MODEL), lambda b, i: (idx, 0, 0)),
    ]
    args += [x, mods, w_out]
    return pl.pallas_call(
        functools.partial(_attn_c_kernel, tq=tq, latent=latent),
        grid=(nb, nq),
        in_specs=in_specs,
        out_specs=x_spec,
        out_shape=jax.ShapeDtypeStruct((N_TOK, D_MODEL), F32),
        input_output_aliases={len(args) - 3: 0},
        compiler_params=_params(2),
        name=f"attn_c_{'lat' if latent else 'ctx'}",
    )(*args)


def kernel(x_prompt, x_sample, cache_b_k, cache_b_v, cache_c_k, cache_c_v, c, c_ctx, w_mod, b_mod, norm_g,
           ffn_w_gate, ffn_w_up, ffn_w_down, a_w_in, a_norm_g, a_w_s, a_b_s, a_w_out, b_w_qkv, b_q_g, b_k_g,
           b_w_out, c_w_qkv, c_sink, c_w_out, final_g):
    x = jnp.concatenate([x_prompt.reshape(N_CTX, D_MODEL), x_sample.reshape(N_LAT, D_MODEL)], axis=0)
    cond = jnp.concatenate([c_ctx[None, :], c, jnp.zeros((MOD_ROWS - 1 - DEC_BATCH, D_MODEL), F32)], axis=0)
    mods = _ada_params(cond, w_mod, b_mod)

    wg, wu, wd = ffn_w_gate.astype(BF16), ffn_w_up.astype(BF16), ffn_w_down.astype(BF16)
    a_w_in_h, a_w_s_h, a_w_out_h = a_w_in.astype(BF16), a_w_s.astype(BF16), a_w_out.astype(BF16)
    a_b_s_t = jnp.swapaxes(a_b_s, 1, 2)
    a_norm_g3 = a_norm_g[:, None, :]
    b_w_qkv_h, b_w_out_h = b_w_qkv.astype(BF16), b_w_out.astype(BF16)
    c_w_qkv_h, c_w_out_h = c_w_qkv.astype(BF16), c_w_out.astype(BF16)
    tables_b = _rope_tables(HD_B)
    tables_c = _rope_tables(HD_C)

    new_kv = {1: [], 2: []}
    for l in range(DEPTH):
        kind, idx = l % 3, l // 3
        x = _ffn(x, mods, norm_g, wg, wu, wd, l, 0)
        if kind == 0:
            x = _mixer_a(x, mods, norm_g, a_w_in_h, a_norm_g3, a_w_s_h, a_b_s_t, a_w_out_h, l, idx)
        elif kind == 1:
            q_g, k_g = b_q_g[:, None, :], b_k_g[:, None, :]
            qc, kc, vc, kf, vf = _qkv_b(x, mods, norm_g, b_w_qkv_h, q_g, k_g, None, l, idx, False)
            ql, kl, vl = _qkv_b(x, mods, norm_g, b_w_qkv_h, q_g, k_g, tables_b, l, idx, True)
            new_kv[1].append((kf.reshape(BATCH, SEQ, KV_B, HD_B), vf.reshape(BATCH, SEQ, KV_B, HD_B)))
            cache = (cache_b_k[:, idx].reshape(DEC_BATCH, PAST_LEN, KV_B * HD_B).astype(BF16),
                     cache_b_v[:, idx].reshape(DEC_BATCH, PAST_LEN, KV_B * HD_B).astype(BF16))
            x = _attn_b(x, qc, kc, vc, None, mods, b_w_out_h, l, idx, False)
            x = _attn_b(x, ql, kl, vl, cache, mods, b_w_out_h, l, idx, True)
        else:
            qc, kc, vc, kf, vf = _qkv_c(x, mods, norm_g, c_w_qkv_h, None, l, idx, False)
            ql, kl, vl = _qkv_c(x, mods, norm_g, c_w_qkv_h, tables_c, l, idx, True)
            new_kv[2].append((kf.reshape(BATCH, SEQ, KV_C, HD_C), vf.reshape(BATCH, SEQ, KV_C, HD_C)))
            cache = (_pad_variants_host(cache_c_k[:, idx]), _pad_variants_host(cache_c_v[:, idx]))
            sink = c_sink[idx]
            x = _attn_c(x, qc, kc, vc, None, sink, mods, c_w_out_h, l, idx, False)
            x = _attn_c(x, ql, kl, vl, cache, sink, mods, c_w_out_h, l, idx, True)
        if l < DEPTH - 1:
            x = _ffn(x, mods, norm_g, wg, wu, wd, l, 1)
    l = DEPTH - 1
    y_prompt = _ffn(x, mods, norm_g, wg, wu, wd, l, 1, final_g=final_g, rows=(0, N_CTX))
    y_sample = _ffn(x, mods, norm_g, wg, wu, wd, l, 1, final_g=final_g, rows=(N_CTX, N_TOK))
    new_b_k = jnp.stack([kv[0] for kv in new_kv[1]], axis=1)
    new_b_v = jnp.stack([kv[1] for kv in new_kv[1]], axis=1)
    new_c_k = jnp.stack([kv[0] for kv in new_kv[2]], axis=1)
    new_c_v = jnp.stack([kv[1] for kv in new_kv[2]], axis=1)
    return (y_prompt.reshape(BATCH, SEQ, D_MODEL), y_sample.reshape(DEC_BATCH, DEC_SEQ, D_MODEL),
            new_b_k, new_b_v, new_c_k, new_c_v)
```

```python
import functools

import jax
import jax.numpy as jnp
from jax import lax
from jax.experimental import pallas as pl
from jax.experimental.pallas import tpu as pltpu

F32 = jnp.float32
BF16 = jnp.bfloat16

D_MODEL = 1024
BATCH = 16
SEQ = 256
DEPTH = 4
DEC_BATCH = 4
DEC_SEQ = 4096
PAST_LEN = 512
GRID_W = 64
N_MOD = 9
D_FF = 2816
CHUNK = 128
D_A = 2 * D_MODEL
G_A = 8
C_A = D_A // G_A
HQ_B, KV_B, HD_B = 8, 2, 128
HQ_C, KV_C, HD_C = 16, 2, 64
WINDOW = 128
ROPE_THETA = 10000.0
EPS = 1e-6
NEG_INF = -1e30
LOG2E = 1.4426950408889634

N_CTX = BATCH * SEQ
N_LAT = DEC_BATCH * DEC_SEQ
N_TOK = N_CTX + N_LAT
GROUP_TOK = 4096
MOD_ROWS = 8
LANES = 128

TM = 512
TQ_LAT = 128
ATTN_TK = 512
BAND = 3 * WINDOW
VMEM_LIMIT = 56 * 1024 * 1024


def _params(n_grid):
    return pltpu.CompilerParams(dimension_semantics=("arbitrary",) * n_grid,
                                vmem_limit_bytes=VMEM_LIMIT)


def _resident(block_shape, index_map):
    return pl.BlockSpec(block_shape, index_map, pipeline_mode=pl.Buffered(1))


def _dot(a, b):
    return jnp.dot(a, b, preferred_element_type=F32)


def _dot_nt(a, b):
    return lax.dot_general(a, b, (((1,), (1,)), ((), ())), preferred_element_type=F32)


def _dot_tn(a, b):
    return lax.dot_general(a, b, (((0,), (0,)), ((), ())), preferred_element_type=F32)


def _rms(x):
    return x * lax.rsqrt(jnp.mean(x * x, axis=-1, keepdims=True) + EPS)


def _norm_mod(x, g, shift, scale):
    return (_rms(x) * g) * (1.0 + scale) + shift


ADA_TN = 2304


def _ada_kernel(cond_ref, w_ref, b_ref, o_ref):
    s = jax.nn.silu(cond_ref[...]).astype(BF16)
    o_ref[...] = _dot(s, w_ref[...].astype(BF16)) + b_ref[...]


def _ada_params(cond, w_mod, b_mod):
    n = N_MOD * D_MODEL
    out = pl.pallas_call(
        _ada_kernel,
        grid=(DEPTH, n // ADA_TN),
        in_specs=[
            pl.BlockSpec((MOD_ROWS, D_MODEL), lambda l, j: (0, 0)),
            pl.BlockSpec((None, D_MODEL, ADA_TN), lambda l, j: (l, 0, j)),
            pl.BlockSpec((None, 1, ADA_TN), lambda l, j: (l, 0, j)),
        ],
        out_specs=pl.BlockSpec((None, MOD_ROWS, ADA_TN), lambda l, j: (l, 0, j)),
        out_shape=jax.ShapeDtypeStruct((DEPTH, MOD_ROWS, n), F32),
        compiler_params=_params(2),
        name="ada_params",
    )(cond, w_mod, b_mod.reshape(DEPTH, 1, n))
    return out.reshape(DEPTH, MOD_ROWS, N_MOD, D_MODEL)


def _ffn_kernel(x_ref, mod_ref, g_ref, wg_ref, wu_ref, wd_ref, *rest, k, gi, final):
    x = x_ref[...]
    h = _norm_mod(x, g_ref[gi:gi + 1, :], mod_ref[k:k + 1, :], mod_ref[k + 1:k + 2, :]).astype(BF16)
    a = (jax.nn.silu(_dot(h, wg_ref[...])) * _dot(h, wu_ref[...])).astype(BF16)
    y = x + (0.5 * mod_ref[k + 2:k + 3, :]) * _dot(a, wd_ref[...])
    if final:
        fg_ref, o_ref = rest
        y = _rms(y) * fg_ref[...]
    else:
        (o_ref,) = rest
    o_ref[...] = y


def _ffn(x, mods, norm_g, wg, wu, wd, l, half, *, final_g=None, rows=None):
    k, gi = (0, 0) if half == 0 else (6, 2)
    r0, r1 = (0, N_TOK) if rows is None else rows
    t0, nt = r0 // TM, (r1 - r0) // TM
    per_group = GROUP_TOK // TM
    final = final_g is not None
    in_specs = [
        pl.BlockSpec((TM, D_MODEL), lambda i: (t0 + i, 0)),
        pl.BlockSpec((None, None, N_MOD, D_MODEL), lambda i: (l, (t0 + i) // per_group, 0, 0)),
        pl.BlockSpec((None, 3, D_MODEL), lambda i: (l, 0, 0)),
        _resident((None, None, D_MODEL, D_FF), lambda i: (l, half, 0, 0)),
        _resident((None, None, D_MODEL, D_FF), lambda i: (l, half, 0, 0)),
        _resident((None, None, D_FF, D_MODEL), lambda i: (l, half, 0, 0)),
    ]
    args = [x, mods, norm_g, wg, wu, wd]
    if final:
        in_specs.append(pl.BlockSpec((1, D_MODEL), lambda i: (0, 0)))
        args.append(final_g.reshape(1, D_MODEL))
    return pl.pallas_call(
        functools.partial(_ffn_kernel, k=k, gi=gi, final=final),
        grid=(nt,),
        in_specs=in_specs,
        out_specs=pl.BlockSpec((TM, D_MODEL), lambda i: (i, 0)),
        out_shape=jax.ShapeDtypeStruct((nt * TM, D_MODEL), F32),
        compiler_params=_params(1),
        name=f"ffn_l{l}_h{half}" + ("_final" if final else ""),
    )(*args)


def _sgu_kernel(x_ref, mod_ref, g_ref, win_ref, ng_ref, ws_ref, bs_ref, wout_ref, o_ref, gated_ref):
    x = x_ref[...]
    h = _norm_mod(x, g_ref[1:2, :], mod_ref[3:4, :], mod_ref[4:5, :]).astype(BF16)
    uv = _dot(h, win_ref[...])
    v = (_rms(uv[:, D_A:]) * ng_ref[...]).astype(BF16)
    for c in range(TM // CHUNK):
        rows = slice(c * CHUNK, (c + 1) * CHUNK)
        for g in range(G_A):
            cols = slice(g * C_A, (g + 1) * C_A)
            mixed = _dot(ws_ref[g], v[rows, cols]) + bs_ref[:, g:g + 1]
            gated_ref[rows, cols] = (uv[rows, cols] * mixed).astype(BF16)
    o_ref[...] = x + mod_ref[5:6, :] * _dot(gated_ref[...], wout_ref[...])


def _mixer_a(x, mods, norm_g, w_in, a_norm_g, w_s, b_s_t, w_out, l, idx):
    per_group = GROUP_TOK // TM
    return pl.pallas_call(
        _sgu_kernel,
        grid=(N_TOK // TM,),
        in_specs=[
            pl.BlockSpec((TM, D_MODEL), lambda i: (i, 0)),
            pl.BlockSpec((None, None, N_MOD, D_MODEL), lambda i: (l, i // per_group, 0, 0)),
            pl.BlockSpec((None, 3, D_MODEL), lambda i: (l, 0, 0)),
            _resident((None, D_MODEL, 2 * D_A), lambda i: (idx, 0, 0)),
            pl.BlockSpec((None, 1, D_A), lambda i: (idx, 0, 0)),
            pl.BlockSpec((None, G_A, CHUNK, CHUNK), lambda i: (idx, 0, 0, 0)),
            pl.BlockSpec((None, CHUNK, G_A), lambda i: (idx, 0, 0)),
            _resident((None, D_A, D_MODEL), lambda i: (idx, 0, 0)),
        ],
        out_specs=pl.BlockSpec((TM, D_MODEL), lambda i: (i, 0)),
        out_shape=jax.ShapeDtypeStruct((N_TOK, D_MODEL), F32),
        scratch_shapes=[pltpu.VMEM((TM, D_A), BF16)],
        compiler_params=_params(1),
        name=f"mixer_a_l{l}",
    )(x, mods, norm_g, w_in, a_norm_g, w_s, b_s_t, w_out)


def _swap_halves(x, half):
    lane = lax.broadcasted_iota(jnp.int32, x.shape, 1)
    first = (lane & (2 * half - 1)) < half
    return jnp.where(first, pltpu.roll(x, LANES - half, 1), pltpu.roll(x, half, 1))


def _rope(x, cos, sin, quarter):
    return x * cos + _swap_halves(x, quarter) * sin


def _rope_tables(hd):
    rows = DEC_SEQ // GRID_W
    row = jnp.repeat(jnp.arange(rows), GRID_W).astype(F32)
    col = (jnp.arange(DEC_SEQ) % GRID_W).astype(F32)
    quarter = hd // 4
    inv = ROPE_THETA ** (-jnp.arange(quarter, dtype=F32) / quarter)
    ang_r = row[:, None] * inv[None, :]
    ang_c = col[:, None] * inv[None, :]
    cos = jnp.concatenate([jnp.cos(ang_r)] * 2 + [jnp.cos(ang_c)] * 2, axis=-1)
    sin = jnp.concatenate([-jnp.sin(ang_r), jnp.sin(ang_r), -jnp.sin(ang_c), jnp.sin(ang_c)], axis=-1)
    reps = LANES // hd
    return jnp.tile(cos, (1, reps)), jnp.tile(sin, (1, reps))


def _attention_pipeline(chains, order, s_buf, p_buf, acc_ref):
    n = len(order)
    state = [ch[2] for ch in chains]
    rows_of, alpha_of = [None] * n, [None] * n
    left = [len(ch[1]) for ch in chains]
    outs = [None] * len(chains)

    def scores(t):
        c, b = order[t]
        k_fn, _, mask = chains[c][1][b]
        st = _dot_nt(k_fn(), chains[c][0]())
        if mask is not None:
            st = jnp.where(mask, st, NEG_INF)
        rows_of[t] = st.shape[0]
        s_buf[t % 2, :st.shape[0]] = st

    def softmax(t):
        c, _ = order[t]
        m, l = state[c]
        st = s_buf[t % 2, :rows_of[t]]
        m_new = jnp.maximum(m, jnp.max(st, axis=0, keepdims=True))
        alpha_of[t] = jnp.exp2(m - m_new)
        p = jnp.exp2(st - m_new)
        p_buf[t % 2, :rows_of[t]] = p.astype(BF16)
        state[c] = (m_new, alpha_of[t] * l + jnp.sum(p, axis=0, keepdims=True))

    def weighted_values(t):
        c, b = order[t]
        pv = _dot_tn(chains[c][1][b][1](), p_buf[t % 2, :rows_of[t]])
        acc_ref[c] = alpha_of[t] * acc_ref[c] + pv
        left[c] -= 1
        if left[c] == 0:
            outs[c] = acc_ref[c] / state[c][1]

    acc_ref[...] = jnp.zeros_like(acc_ref)
    for t in range(-1, n + 1):
        if t + 1 < n:
            scores(t + 1)
        if 0 <= t < n:
            softmax(t)
        if 0 <= t - 1:
            weighted_values(t - 1)
    return outs


def _attention_scratch(n_chains, n_max, nq, hd):
    return [pltpu.VMEM((2, n_max, nq), F32), pltpu.VMEM((2, n_max, nq), BF16),
            pltpu.VMEM((n_chains, hd, nq), F32)]


def _block_major(chains_blocks):
    depth = max(chains_blocks)
    return [(c, b) for b in range(depth) for c, nb in enumerate(chains_blocks) if b < nb]


QKV_B = (HQ_B + 2 * KV_B) * HD_B
SKV_B_LAT = PAST_LEN + DEC_SEQ


def _qkv_b_kernel(x_ref, mod_ref, g_ref, w_ref, qg_ref, kg_ref, *rest, latent):
    if latent:
        cos_ref, sin_ref, kc_ref, vc_ref, q_ref, k_ref, v_ref = rest
    else:
        q_ref, k_ref, v_ref, kf_ref, vf_ref = rest

    def project():
        h = _norm_mod(x_ref[...], g_ref[1:2, :], mod_ref[3:4, :], mod_ref[4:5, :]).astype(BF16)
        qkv = _dot(h, w_ref[...])
        scale = HD_B ** -0.5 * LOG2E
        for j in range(HQ_B + KV_B):
            is_q = j < HQ_B
            y = _rms(qkv[:, j * HD_B:(j + 1) * HD_B]) * (qg_ref[...] if is_q else kg_ref[...])
            kcols = slice((j - HQ_B) * HD_B, (j - HQ_B + 1) * HD_B)
            if not latent and not is_q:
                kf_ref[:, kcols] = y
            if latent:
                y = _rope(y, cos_ref[...], sin_ref[...], HD_B // 4)
            if is_q:
                q_ref[j] = (y * scale).astype(BF16)
            else:
                k_ref[:, kcols] = y.astype(BF16)
        v = qkv[:, (HQ_B + KV_B) * HD_B:]
        v_ref[...] = v.astype(BF16)
        if not latent:
            vf_ref[...] = v

    if latent:
        t = pl.program_id(1)

        @pl.when(t == 0)
        def _():
            k_ref[...] = kc_ref[...].astype(BF16)
            v_ref[...] = vc_ref[...].astype(BF16)

        pl.when(t > 0)(project)
    else:
        project()


def _qkv_specs(latent, l):
    per_batch = DEC_SEQ // TM
    if latent:
        assert PAST_LEN == TM
        grid = (DEC_BATCH, 1 + per_batch)

        def tile(b, t):
            return b * per_batch + jnp.maximum(t - 1, 0)

        x_map = lambda b, t: (N_CTX // TM + tile(b, t), 0)
        mod_map = lambda b, t: (l, 1 + b, 0, 0)
        q_map = lambda b, t: (0, tile(b, t), 0)
        kv_map = lambda b, t: (b * (1 + per_batch) + t, 0)
        pos_map = lambda b, t: (jnp.maximum(t - 1, 0), 0)
        return grid, N_LAT, DEC_BATCH * SKV_B_LAT, x_map, mod_map, q_map, kv_map, pos_map
    x_map = lambda i: (i, 0)
    return ((N_CTX // TM,), N_CTX, N_CTX, x_map, lambda i: (l, 0, 0, 0), lambda i: (0, i, 0), x_map, None)


def _qkv_b(x, mods, norm_g, w_qkv, q_g, k_g, tables, cache, l, idx, latent):
    kvw = KV_B * HD_B
    grid, n, kv_rows, x_map, mod_map, q_map, kv_map, pos_map = _qkv_specs(latent, l)
    const = lambda *_: (idx, 0, 0)
    in_specs = [
        pl.BlockSpec((TM, D_MODEL), x_map),
        pl.BlockSpec((None, None, N_MOD, D_MODEL), mod_map),
        pl.BlockSpec((None, 3, D_MODEL), lambda *_: (l, 0, 0)),
        _resident((None, D_MODEL, QKV_B), const),
        pl.BlockSpec((None, 1, HD_B), const),
        pl.BlockSpec((None, 1, HD_B), const),
    ]
    args = [x, mods, norm_g, w_qkv, q_g, k_g]
    out_specs = [
        pl.BlockSpec((HQ_B, TM, HD_B), q_map),
        pl.BlockSpec((TM, kvw), kv_map),
        pl.BlockSpec((TM, kvw), kv_map),
    ]
    out_shape = [jax.ShapeDtypeStruct((HQ_B, n, HD_B), BF16), jax.ShapeDtypeStruct((kv_rows, kvw), BF16),
                 jax.ShapeDtypeStruct((kv_rows, kvw), BF16)]
    if latent:
        in_specs += [pl.BlockSpec((TM, LANES), pos_map)] * 2
        in_specs += [pl.BlockSpec((None, None, PAST_LEN, kvw), lambda b, t: (b, idx, 0, 0))] * 2
        args += list(tables) + list(cache)
    else:
        out_specs += [pl.BlockSpec((TM, kvw), kv_map)] * 2
        out_shape += [jax.ShapeDtypeStruct((n, kvw), F32)] * 2
    return pl.pallas_call(
        functools.partial(_qkv_b_kernel, latent=latent),
        grid=grid,
        in_specs=in_specs,
        out_specs=out_specs,
        out_shape=out_shape,
        compiler_params=_params(len(grid)),
        name=f"qkv_b_{'lat' if latent else 'ctx'}",
    )(*args)


def _attn_b_kernel(q_ref, k_ref, v_ref, x_ref, mod_ref, wout_ref, o_ref, s_buf, p_buf, acc_ref, *, tq, tk):
    group = HQ_B // KV_B
    nq = group * tq
    nblk = k_ref.shape[0] // tk
    row = jnp.zeros((1, nq), F32)

    def chain(kh):
        cols = slice(kh * HD_B, (kh + 1) * HD_B)
        q_fn = lambda: q_ref[kh * group:(kh + 1) * group].reshape(nq, HD_B)
        blocks = [(lambda j=j: k_ref[j * tk:(j + 1) * tk, cols], lambda j=j: v_ref[j * tk:(j + 1) * tk, cols], None)
                  for j in range(nblk)]
        return q_fn, blocks, (row + NEG_INF, row)

    outs = _attention_pipeline([chain(kh) for kh in range(KV_B)], _block_major([nblk] * KV_B),
                               s_buf, p_buf, acc_ref)
    heads = [outs[kh][:, g * tq:(g + 1) * tq].T for kh in range(KV_B) for g in range(group)]
    attn = jnp.concatenate(heads, axis=1).astype(BF16)
    o_ref[...] = x_ref[...] + mod_ref[5:6, :] * _dot(attn, wout_ref[...])


def _attn_geometry(latent):
    if latent:
        return TQ_LAT, DEC_BATCH, DEC_SEQ // TQ_LAT, N_CTX // TQ_LAT
    return SEQ, BATCH, 1, 0


def _attn_b(x, q, k, v, mods, w_out, l, idx, latent):
    kvw = KV_B * HD_B
    tq, nb, nqb, x0 = _attn_geometry(latent)
    skv = SKV_B_LAT if latent else SEQ
    tk = min(ATTN_TK, skv)
    nq = (HQ_B // KV_B) * tq
    per_group = GROUP_TOK // tq
    x_spec = pl.BlockSpec((tq, D_MODEL), lambda b, i: (x0 + b * nqb + i, 0))
    in_specs = [
        pl.BlockSpec((HQ_B, tq, HD_B), lambda b, i: (0, b * nqb + i, 0)),
        pl.BlockSpec((skv, kvw), lambda b, i: (b, 0)),
        pl.BlockSpec((skv, kvw), lambda b, i: (b, 0)),
        x_spec,
        pl.BlockSpec((None, None, N_MOD, D_MODEL), lambda b, i: (l, (x0 + b * nqb + i) // per_group, 0, 0)),
        _resident((None, HQ_B * HD_B, D_MODEL), lambda b, i: (idx, 0, 0)),
    ]
    return pl.pallas_call(
        functools.partial(_attn_b_kernel, tq=tq, tk=tk),
        grid=(nb, nqb),
        in_specs=in_specs,
        out_specs=x_spec,
        out_shape=jax.ShapeDtypeStruct((N_TOK, D_MODEL), F32),
        scratch_shapes=_attention_scratch(KV_B, tk, nq, HD_B),
        input_output_aliases={3: 0},
        compiler_params=_params(2),
        name=f"attn_b_{'lat' if latent else 'ctx'}",
    )(q, k, v, x, mods, w_out)


QKV_C = (HQ_C + 2 * KV_C) * HD_C
Q_SLABS_C = HQ_C * HD_C // LANES
KV_PAD_C = 4 * LANES
SKV_C_LAT = PAST_LEN + DEC_SEQ


def _pad_variants(y):
    lane = lax.broadcasted_iota(jnp.int32, y.shape, 1)
    low = lane < HD_C
    swapped = pltpu.roll(y, HD_C, 1)
    zero = jnp.zeros_like(y)
    return jnp.concatenate([jnp.where(low, y, zero), jnp.where(low, zero, swapped),
                            jnp.where(low, swapped, zero), jnp.where(low, zero, y)], axis=1)


def _qkv_c_kernel(x_ref, mod_ref, g_ref, w_ref, *rest, latent):
    if latent:
        cos_ref, sin_ref, kc_ref, vc_ref, q_ref, k_ref, v_ref = rest
    else:
        q_ref, k_ref, v_ref, kf_ref, vf_ref = rest

    def project():
        h = _norm_mod(x_ref[...], g_ref[1:2, :], mod_ref[3:4, :], mod_ref[4:5, :]).astype(BF16)
        qkv = _dot(h, w_ref[...])
        scale = HD_C ** -0.5 * LOG2E
        for j in range(Q_SLABS_C):
            y = qkv[:, j * LANES:(j + 1) * LANES]
            if latent:
                y = _rope(y, cos_ref[...], sin_ref[...], HD_C // 4)
            q_ref[j] = (y * scale).astype(BF16)
        k = qkv[:, D_MODEL:D_MODEL + LANES]
        v = qkv[:, D_MODEL + LANES:]
        if latent:
            k = _rope(k, cos_ref[...], sin_ref[...], HD_C // 4)
        else:
            kf_ref[...] = k
            vf_ref[...] = v
        k_ref[...] = _pad_variants(k).astype(BF16)
        v_ref[...] = _pad_variants(v).astype(BF16)

    if latent:
        t = pl.program_id(1)

        @pl.when(t == 0)
        def _():
            k_ref[...] = _pad_variants(kc_ref[...]).astype(BF16)
            v_ref[...] = _pad_variants(vc_ref[...]).astype(BF16)

        pl.when(t > 0)(project)
    else:
        project()


def _qkv_c(x, mods, norm_g, w_qkv, tables, cache, l, idx, latent):
    grid, n, kv_rows, x_map, mod_map, q_map, kv_map, pos_map = _qkv_specs(latent, l)
    in_specs = [
        pl.BlockSpec((TM, D_MODEL), x_map),
        pl.BlockSpec((None, None, N_MOD, D_MODEL), mod_map),
        pl.BlockSpec((None, 3, D_MODEL), lambda *_: (l, 0, 0)),
        _resident((None, D_MODEL, QKV_C), lambda *_: (idx, 0, 0)),
    ]
    args = [x, mods, norm_g, w_qkv]
    out_specs = [
        pl.BlockSpec((Q_SLABS_C, TM, LANES), q_map),
        pl.BlockSpec((TM, KV_PAD_C), kv_map),
        pl.BlockSpec((TM, KV_PAD_C), kv_map),
    ]
    out_shape = [jax.ShapeDtypeStruct((Q_SLABS_C, n, LANES), BF16), jax.ShapeDtypeStruct((kv_rows, KV_PAD_C), BF16),
                 jax.ShapeDtypeStruct((kv_rows, KV_PAD_C), BF16)]
    if latent:
        in_specs += [pl.BlockSpec((TM, LANES), pos_map)] * 2
        in_specs += [pl.BlockSpec((None, None, PAST_LEN, LANES), lambda b, t: (b, idx, 0, 0))] * 2
        args += list(tables) + list(cache)
    else:
        out_specs += [pl.BlockSpec((TM, LANES), kv_map)] * 2
        out_shape += [jax.ShapeDtypeStruct((n, LANES), F32)] * 2
    return pl.pallas_call(
        functools.partial(_qkv_c_kernel, latent=latent),
        grid=grid,
        in_specs=in_specs,
        out_specs=out_specs,
        out_shape=out_shape,
        compiler_params=_params(len(grid)),
        name=f"qkv_c_{'lat' if latent else 'ctx'}",
    )(*args)


def _attn_c_kernel(sink_ref, q_ref, k_ref, v_ref, x_ref, mod_ref, wout_ref, o_ref, s_buf, p_buf, acc_ref,
                   *, tq, latent):
    group = HQ_C // KV_C
    pairs = group // 2
    nq = pairs * tq
    if latent:
        i = pl.program_id(1)
        start = pl.multiple_of(jnp.clip(i * tq - WINDOW, 0, DEC_SEQ - BAND), WINDOW)
        band = pl.ds(pl.multiple_of(PAST_LEN + start, WINDOW), BAND)
        qpos = i * tq + (lax.broadcasted_iota(jnp.int32, (BAND, nq), 1) & (tq - 1))
        kpos = start + lax.broadcasted_iota(jnp.int32, (BAND, nq), 0)
        near = jnp.abs(qpos - kpos) <= WINDOW
        spans = [(slice(0, PAST_LEN), None), (band, near)]
    else:
        spans = [(slice(None), None)]

    def chain(kh, par):
        cols = slice((2 * kh + par) * LANES, (2 * kh + par + 1) * LANES)
        q_fn = lambda: q_ref[kh * pairs:(kh + 1) * pairs].reshape(nq, LANES)
        blocks = [(lambda r=r: k_ref[r, cols], lambda r=r: v_ref[r, cols], mask) for r, mask in spans]
        sink = jnp.concatenate([jnp.full((1, tq), sink_ref[kh * group + 2 * j + par] * LOG2E, F32)
                                for j in range(pairs)], axis=1)
        return q_fn, blocks, (sink, jnp.ones_like(sink))

    chains = [chain(kh, par) for kh in range(KV_C) for par in range(2)]
    outs = _attention_pipeline(chains, _block_major([len(spans)] * len(chains)), s_buf, p_buf, acc_ref)
    slabs = []
    for kh in range(KV_C):
        o_t = outs[2 * kh] + outs[2 * kh + 1]
        slabs += [o_t[:, j * tq:(j + 1) * tq].T for j in range(pairs)]
    attn = jnp.concatenate(slabs, axis=1).astype(BF16)
    o_ref[...] = x_ref[...] + mod_ref[5:6, :] * _dot(attn, wout_ref[...])


def _attn_c(x, q, k, v, sink, mods, w_out, l, idx, latent):
    tq, nb, nqb, x0 = _attn_geometry(latent)
    skv = SKV_C_LAT if latent else SEQ
    nq = (HQ_C // KV_C // 2) * tq
    per_group = GROUP_TOK // tq
    x_spec = pl.BlockSpec((tq, D_MODEL), lambda b, i: (x0 + b * nqb + i, 0))
    in_specs = [
        pl.BlockSpec(memory_space=pltpu.SMEM),
        pl.BlockSpec((Q_SLABS_C, tq, LANES), lambda b, i: (0, b * nqb + i, 0)),
        pl.BlockSpec((skv, KV_PAD_C), lambda b, i: (b, 0)),
        pl.BlockSpec((skv, KV_PAD_C), lambda b, i: (b, 0)),
        x_spec,
        pl.BlockSpec((None, None, N_MOD, D_MODEL), lambda b, i: (l, (x0 + b * nqb + i) // per_group, 0, 0)),
        _resident((None, HQ_C * HD_C, D_MODEL), lambda b, i: (idx, 0, 0)),
    ]
    return pl.pallas_call(
        functools.partial(_attn_c_kernel, tq=tq, latent=latent),
        grid=(nb, nqb),
        in_specs=in_specs,
        out_specs=x_spec,
        out_shape=jax.ShapeDtypeStruct((N_TOK, D_MODEL), F32),
        scratch_shapes=_attention_scratch(2 * KV_C, PAST_LEN if latent else SEQ, nq, LANES),
        input_output_aliases={4: 0},
        compiler_params=_params(2),
        name=f"attn_c_{'lat' if latent else 'ctx'}",
    )(sink, q, k, v, x, mods, w_out)


def kernel(x_prompt, x_sample, cache_b_k, cache_b_v, cache_c_k, cache_c_v, c, c_ctx, w_mod, b_mod, norm_g,
           ffn_w_gate, ffn_w_up, ffn_w_down, a_w_in, a_norm_g, a_w_s, a_b_s, a_w_out, b_w_qkv, b_q_g, b_k_g,
           b_w_out, c_w_qkv, c_sink, c_w_out, final_g):
    x = jnp.concatenate([x_prompt.reshape(N_CTX, D_MODEL), x_sample.reshape(N_LAT, D_MODEL)], axis=0)
    cond = jnp.concatenate([c_ctx[None, :], c, jnp.zeros((MOD_ROWS - 1 - DEC_BATCH, D_MODEL), F32)], axis=0)
    mods = _ada_params(cond, w_mod, b_mod)

    wg, wu, wd = ffn_w_gate.astype(BF16), ffn_w_up.astype(BF16), ffn_w_down.astype(BF16)
    a_w_in_h, a_w_s_h, a_w_out_h = a_w_in.astype(BF16), a_w_s.astype(BF16), a_w_out.astype(BF16)
    a_b_s_t = jnp.swapaxes(a_b_s, 1, 2)
    a_norm_g3 = a_norm_g[:, None, :]
    b_w_qkv_h, b_w_out_h = b_w_qkv.astype(BF16), b_w_out.astype(BF16)
    c_w_qkv_h, c_w_out_h = c_w_qkv.astype(BF16), c_w_out.astype(BF16)
    tables_b = _rope_tables(HD_B)
    tables_c = _rope_tables(HD_C)
    cache_b = (cache_b_k.reshape(DEC_BATCH, -1, PAST_LEN, KV_B * HD_B),
               cache_b_v.reshape(DEC_BATCH, -1, PAST_LEN, KV_B * HD_B))
    cache_c = (cache_c_k.reshape(DEC_BATCH, -1, PAST_LEN, KV_C * HD_C),
               cache_c_v.reshape(DEC_BATCH, -1, PAST_LEN, KV_C * HD_C))

    new_kv = {1: [], 2: []}
    for l in range(DEPTH):
        kind, idx = l % 3, l // 3
        x = _ffn(x, mods, norm_g, wg, wu, wd, l, 0)
        if kind == 0:
            x = _mixer_a(x, mods, norm_g, a_w_in_h, a_norm_g3, a_w_s_h, a_b_s_t, a_w_out_h, l, idx)
        elif kind == 1:
            q_g, k_g = b_q_g[:, None, :], b_k_g[:, None, :]
            qc, kc, vc, kf, vf = _qkv_b(x, mods, norm_g, b_w_qkv_h, q_g, k_g, None, None, l, idx, False)
            ql, kl, vl = _qkv_b(x, mods, norm_g, b_w_qkv_h, q_g, k_g, tables_b, cache_b, l, idx, True)
            new_kv[1].append((kf.reshape(BATCH, SEQ, KV_B, HD_B), vf.reshape(BATCH, SEQ, KV_B, HD_B)))
            x = _attn_b(x, qc, kc, vc, mods, b_w_out_h, l, idx, False)
            x = _attn_b(x, ql, kl, vl, mods, b_w_out_h, l, idx, True)
        else:
            qc, kc, vc, kf, vf = _qkv_c(x, mods, norm_g, c_w_qkv_h, None, None, l, idx, False)
            ql, kl, vl = _qkv_c(x, mods, norm_g, c_w_qkv_h, tables_c, cache_c, l, idx, True)
            new_kv[2].append((kf.reshape(BATCH, SEQ, KV_C, HD_C), vf.reshape(BATCH, SEQ, KV_C, HD_C)))
            sink = c_sink[idx]
            x = _attn_c(x, qc, kc, vc, sink, mods, c_w_out_h, l, idx, False)
            x = _attn_c(x, ql, kl, vl, sink, mods, c_w_out_h, l, idx, True)
        if l < DEPTH - 1:
            x = _ffn(x, mods, norm_g, wg, wu, wd, l, 1)
    l = DEPTH - 1
    y_prompt = _ffn(x, mods, norm_g, wg, wu, wd, l, 1, final_g=final_g, rows=(0, N_CTX))
    y_sample = _ffn(x, mods, norm_g, wg, wu, wd, l, 1, final_g=final_g, rows=(N_CTX, N_TOK))
    new_b_k = jnp.stack([kv[0] for kv in new_kv[1]], axis=1)
    new_b_v = jnp.stack([kv[1] for kv in new_kv[1]], axis=1)
    new_c_k = jnp.stack([kv[0] for kv in new_kv[2]], axis=1)
    new_c_v = jnp.stack([kv[1] for kv in new_kv[2]], axis=1)
    return (y_prompt.reshape(BATCH, SEQ, D_MODEL), y_sample.reshape(DEC_BATCH, DEC_SEQ, D_MODEL),
            new_b_k, new_b_v, new_c_k, new_c_v)
```

```python
import functools

import jax
import jax.numpy as jnp
from jax import lax
from jax.experimental import pallas as pl
from jax.experimental.pallas import tpu as pltpu

F32 = jnp.float32
BF16 = jnp.bfloat16

D_MODEL = 1024
BATCH = 16
SEQ = 256
DEPTH = 4
DEC_BATCH = 4
DEC_SEQ = 4096
PAST_LEN = 512
GRID_W = 64
N_MOD = 9
D_FF = 2816
CHUNK = 128
D_A = 2 * D_MODEL
G_A = 8
C_A = D_A // G_A
HQ_B, KV_B, HD_B = 8, 2, 128
HQ_C, KV_C, HD_C = 16, 2, 64
WINDOW = 128
ROPE_THETA = 10000.0
EPS = 1e-6
NEG_INF = -1e30
LOG2E = 1.4426950408889634

N_CTX = BATCH * SEQ
N_LAT = DEC_BATCH * DEC_SEQ
N_TOK = N_CTX + N_LAT
GROUP_TOK = 4096
MOD_ROWS = 8
LANES = 128

TM = 512
TQ_LAT = 128
ATTN_TK = 512
BAND = 3 * WINDOW
VMEM_LIMIT = 56 * 1024 * 1024


def _params(n_grid):
    return pltpu.CompilerParams(dimension_semantics=("arbitrary",) * n_grid,
                                vmem_limit_bytes=VMEM_LIMIT)


def _resident(block_shape, index_map):
    return pl.BlockSpec(block_shape, index_map, pipeline_mode=pl.Buffered(1))


def _dot(a, b):
    return jnp.dot(a, b, preferred_element_type=F32)


def _dot_nt(a, b):
    return lax.dot_general(a, b, (((1,), (1,)), ((), ())), preferred_element_type=F32)


def _dot_tn(a, b):
    return lax.dot_general(a, b, (((0,), (0,)), ((), ())), preferred_element_type=F32)


def _rms(x):
    return x * lax.rsqrt(jnp.mean(x * x, axis=-1, keepdims=True) + EPS)


def _norm_mod(x, g, shift, scale):
    return (_rms(x) * g) * (1.0 + scale) + shift


ADA_TN = 2304


def _ada_kernel(cond_ref, w_ref, b_ref, o_ref):
    s = jax.nn.silu(cond_ref[...]).astype(BF16)
    o_ref[...] = _dot(s, w_ref[...].astype(BF16)) + b_ref[...]


def _ada_params(cond, w_mod, b_mod):
    n = N_MOD * D_MODEL
    out = pl.pallas_call(
        _ada_kernel,
        grid=(DEPTH, n // ADA_TN),
        in_specs=[
            pl.BlockSpec((MOD_ROWS, D_MODEL), lambda l, j: (0, 0)),
            pl.BlockSpec((None, D_MODEL, ADA_TN), lambda l, j: (l, 0, j)),
            pl.BlockSpec((None, 1, ADA_TN), lambda l, j: (l, 0, j)),
        ],
        out_specs=pl.BlockSpec((None, MOD_ROWS, ADA_TN), lambda l, j: (l, 0, j)),
        out_shape=jax.ShapeDtypeStruct((DEPTH, MOD_ROWS, n), F32),
        compiler_params=_params(2),
        name="ada_params",
    )(cond, w_mod, b_mod.reshape(DEPTH, 1, n))
    return out.reshape(DEPTH, MOD_ROWS, N_MOD, D_MODEL)


def _ffn_kernel(x_ref, mod_ref, g_ref, wg_ref, wu_ref, wd_ref, *rest, k, gi, final):
    x = x_ref[...]
    h = _norm_mod(x, g_ref[gi:gi + 1, :], mod_ref[k:k + 1, :], mod_ref[k + 1:k + 2, :]).astype(BF16)
    a = (jax.nn.silu(_dot(h, wg_ref[...])) * _dot(h, wu_ref[...])).astype(BF16)
    y = x + (0.5 * mod_ref[k + 2:k + 3, :]) * _dot(a, wd_ref[...])
    if final:
        fg_ref, o_ref = rest
        y = _rms(y) * fg_ref[...]
    else:
        (o_ref,) = rest
    o_ref[...] = y


def _ffn(x, mods, norm_g, wg, wu, wd, l, half, *, final_g=None, rows=None):
    k, gi = (0, 0) if half == 0 else (6, 2)
    r0, r1 = (0, N_TOK) if rows is None else rows
    t0, nt = r0 // TM, (r1 - r0) // TM
    per_group = GROUP_TOK // TM
    final = final_g is not None
    in_specs = [
        pl.BlockSpec((TM, D_MODEL), lambda i: (t0 + i, 0)),
        pl.BlockSpec((None, None, N_MOD, D_MODEL), lambda i: (l, (t0 + i) // per_group, 0, 0)),
        pl.BlockSpec((None, 3, D_MODEL), lambda i: (l, 0, 0)),
        _resident((None, None, D_MODEL, D_FF), lambda i: (l, half, 0, 0)),
        _resident((None, None, D_MODEL, D_FF), lambda i: (l, half, 0, 0)),
        _resident((None, None, D_FF, D_MODEL), lambda i: (l, half, 0, 0)),
    ]
    args = [x, mods, norm_g, wg, wu, wd]
    if final:
        in_specs.append(pl.BlockSpec((1, D_MODEL), lambda i: (0, 0)))
        args.append(final_g.reshape(1, D_MODEL))
    return pl.pallas_call(
        functools.partial(_ffn_kernel, k=k, gi=gi, final=final),
        grid=(nt,),
        in_specs=in_specs,
        out_specs=pl.BlockSpec((TM, D_MODEL), lambda i: (i, 0)),
        out_shape=jax.ShapeDtypeStruct((nt * TM, D_MODEL), F32),
        compiler_params=_params(1),
        name=f"ffn_l{l}_h{half}" + ("_final" if final else ""),
    )(*args)


def _sgu_kernel(x_ref, mod_ref, g_ref, win_ref, ng_ref, ws_ref, bs_ref, wout_ref, o_ref, gated_ref):
    x = x_ref[...]
    h = _norm_mod(x, g_ref[1:2, :], mod_ref[3:4, :], mod_ref[4:5, :]).astype(BF16)
    uv = _dot(h, win_ref[...])
    v = (_rms(uv[:, D_A:]) * ng_ref[...]).astype(BF16)
    for c in range(TM // CHUNK):
        rows = slice(c * CHUNK, (c + 1) * CHUNK)
        for g in range(G_A):
            cols = slice(g * C_A, (g + 1) * C_A)
            mixed = _dot(ws_ref[g], v[rows, cols]) + bs_ref[:, g:g + 1]
            gated_ref[rows, cols] = (uv[rows, cols] * mixed).astype(BF16)
    o_ref[...] = x + mod_ref[5:6, :] * _dot(gated_ref[...], wout_ref[...])


def _mixer_a(x, mods, norm_g, w_in, a_norm_g, w_s, b_s_t, w_out, l, idx):
    per_group = GROUP_TOK // TM
    return pl.pallas_call(
        _sgu_kernel,
        grid=(N_TOK // TM,),
        in_specs=[
            pl.BlockSpec((TM, D_MODEL), lambda i: (i, 0)),
            pl.BlockSpec((None, None, N_MOD, D_MODEL), lambda i: (l, i // per_group, 0, 0)),
            pl.BlockSpec((None, 3, D_MODEL), lambda i: (l, 0, 0)),
            _resident((None, D_MODEL, 2 * D_A), lambda i: (idx, 0, 0)),
            pl.BlockSpec((None, 1, D_A), lambda i: (idx, 0, 0)),
            pl.BlockSpec((None, G_A, CHUNK, CHUNK), lambda i: (idx, 0, 0, 0)),
            pl.BlockSpec((None, CHUNK, G_A), lambda i: (idx, 0, 0)),
            _resident((None, D_A, D_MODEL), lambda i: (idx, 0, 0)),
        ],
        out_specs=pl.BlockSpec((TM, D_MODEL), lambda i: (i, 0)),
        out_shape=jax.ShapeDtypeStruct((N_TOK, D_MODEL), F32),
        scratch_shapes=[pltpu.VMEM((TM, D_A), BF16)],
        compiler_params=_params(1),
        name=f"mixer_a_l{l}",
    )(x, mods, norm_g, w_in, a_norm_g, w_s, b_s_t, w_out)


def _swap_halves(x, half):
    lane = lax.broadcasted_iota(jnp.int32, x.shape, 1)
    first = (lane & (2 * half - 1)) < half
    return jnp.where(first, pltpu.roll(x, LANES - half, 1), pltpu.roll(x, half, 1))


def _rope(x, cos, sin, quarter):
    return x * cos + _swap_halves(x, quarter) * sin


def _rope_tables(hd):
    rows = DEC_SEQ // GRID_W
    row = jnp.repeat(jnp.arange(rows), GRID_W).astype(F32)
    col = (jnp.arange(DEC_SEQ) % GRID_W).astype(F32)
    quarter = hd // 4
    inv = ROPE_THETA ** (-jnp.arange(quarter, dtype=F32) / quarter)
    ang_r = row[:, None] * inv[None, :]
    ang_c = col[:, None] * inv[None, :]
    cos = jnp.concatenate([jnp.cos(ang_r)] * 2 + [jnp.cos(ang_c)] * 2, axis=-1)
    sin = jnp.concatenate([-jnp.sin(ang_r), jnp.sin(ang_r), -jnp.sin(ang_c), jnp.sin(ang_c)], axis=-1)
    reps = LANES // hd
    return jnp.tile(cos, (1, reps)), jnp.tile(sin, (1, reps))


def _attention_pipeline(chains, order, s_buf, p_buf, acc_ref):
    n = len(order)
    state = [ch[2] for ch in chains]
    rows_of, alpha_of = [None] * n, [None] * n
    left = [len(ch[1]) for ch in chains]
    outs = [None] * len(chains)

    def scores(t):
        c, b = order[t]
        k_fn, _, mask = chains[c][1][b]
        st = _dot_nt(k_fn(), chains[c][0]())
        if mask is not None:
            st = jnp.where(mask, st, NEG_INF)
        rows_of[t] = st.shape[0]
        s_buf[t % 2, :st.shape[0]] = st

    def softmax(t):
        c, _ = order[t]
        m, l = state[c]
        st = s_buf[t % 2, :rows_of[t]]
        m_new = jnp.maximum(m, jnp.max(st, axis=0, keepdims=True))
        alpha_of[t] = jnp.exp2(m - m_new)
        p = jnp.exp2(st - m_new)
        p_buf[t % 2, :rows_of[t]] = p.astype(BF16)
        state[c] = (m_new, alpha_of[t] * l + jnp.sum(p, axis=0, keepdims=True))

    def weighted_values(t):
        c, b = order[t]
        pv = _dot(chains[c][1][b][1](), p_buf[t % 2, :rows_of[t]])
        acc_ref[c] = alpha_of[t] * acc_ref[c] + pv
        left[c] -= 1
        if left[c] == 0:
            outs[c] = acc_ref[c] / state[c][1]

    acc_ref[...] = jnp.zeros_like(acc_ref)
    for t in range(-1, n + 1):
        if t + 1 < n:
            scores(t + 1)
        if 0 <= t < n:
            softmax(t)
        if 0 <= t - 1:
            weighted_values(t - 1)
    return outs


def _attention_scratch(n_chains, n_max, nq, hd):
    return [pltpu.VMEM((2, n_max, nq), F32), pltpu.VMEM((2, n_max, nq), BF16),
            pltpu.VMEM((n_chains, hd, nq), F32)]


def _block_major(chains_blocks):
    depth = max(chains_blocks)
    return [(c, b) for b in range(depth) for c, nb in enumerate(chains_blocks) if b < nb]


QKV_B = (HQ_B + 2 * KV_B) * HD_B
SKV_B_LAT = PAST_LEN + DEC_SEQ


def _qkv_b_kernel(x_ref, mod_ref, g_ref, w_ref, qg_ref, kg_ref, *rest, latent):
    if latent:
        cos_ref, sin_ref, kc_ref, vc_ref, q_ref, k_ref, v_ref = rest
    else:
        q_ref, k_ref, v_ref, kf_ref, vf_ref = rest

    def project():
        h = _norm_mod(x_ref[...], g_ref[1:2, :], mod_ref[3:4, :], mod_ref[4:5, :]).astype(BF16)
        qkv = _dot(h, w_ref[...])
        scale = HD_B ** -0.5 * LOG2E
        for j in range(HQ_B + KV_B):
            is_q = j < HQ_B
            y = _rms(qkv[:, j * HD_B:(j + 1) * HD_B]) * (qg_ref[...] if is_q else kg_ref[...])
            kcols = slice((j - HQ_B) * HD_B, (j - HQ_B + 1) * HD_B)
            if not latent and not is_q:
                kf_ref[:, kcols] = y
            if latent:
                y = _rope(y, cos_ref[...], sin_ref[...], HD_B // 4)
            if is_q:
                q_ref[j] = (y * scale).astype(BF16)
            else:
                k_ref[:, kcols] = y.astype(BF16)
        v = qkv[:, (HQ_B + KV_B) * HD_B:]
        v_ref[...] = v.T.astype(BF16)
        if not latent:
            vf_ref[...] = v

    if latent:
        t = pl.program_id(1)

        @pl.when(t == 0)
        def _():
            k_ref[...] = kc_ref[...].astype(BF16)
            v_ref[...] = vc_ref[...].T.astype(BF16)

        pl.when(t > 0)(project)
    else:
        project()


def _qkv_specs(latent, l):
    per_batch = DEC_SEQ // TM
    if latent:
        assert PAST_LEN == TM
        grid = (DEC_BATCH, 1 + per_batch)

        def tile(b, t):
            return b * per_batch + jnp.maximum(t - 1, 0)

        x_map = lambda b, t: (N_CTX // TM + tile(b, t), 0)
        mod_map = lambda b, t: (l, 1 + b, 0, 0)
        q_map = lambda b, t: (0, tile(b, t), 0)
        kv_map = lambda b, t: (b * (1 + per_batch) + t, 0)
        vt_map = lambda b, t: (0, b * (1 + per_batch) + t)
        pos_map = lambda b, t: (jnp.maximum(t - 1, 0), 0)
        return grid, N_LAT, DEC_BATCH * SKV_B_LAT, x_map, mod_map, q_map, kv_map, vt_map, pos_map
    x_map = lambda i: (i, 0)
    return ((N_CTX // TM,), N_CTX, N_CTX, x_map, lambda i: (l, 0, 0, 0), lambda i: (0, i, 0), x_map,
            lambda i: (0, i), None)


def _qkv_b(x, mods, norm_g, w_qkv, q_g, k_g, tables, cache, l, idx, latent):
    kvw = KV_B * HD_B
    grid, n, kv_rows, x_map, mod_map, q_map, kv_map, vt_map, pos_map = _qkv_specs(latent, l)
    const = lambda *_: (idx, 0, 0)
    in_specs = [
        pl.BlockSpec((TM, D_MODEL), x_map),
        pl.BlockSpec((None, None, N_MOD, D_MODEL), mod_map),
        pl.BlockSpec((None, 3, D_MODEL), lambda *_: (l, 0, 0)),
        _resident((None, D_MODEL, QKV_B), const),
        pl.BlockSpec((None, 1, HD_B), const),
        pl.BlockSpec((None, 1, HD_B), const),
    ]
    args = [x, mods, norm_g, w_qkv, q_g, k_g]
    out_specs = [
        pl.BlockSpec((HQ_B, TM, HD_B), q_map),
        pl.BlockSpec((TM, kvw), kv_map),
        pl.BlockSpec((kvw, TM), vt_map),
    ]
    out_shape = [jax.ShapeDtypeStruct((HQ_B, n, HD_B), BF16), jax.ShapeDtypeStruct((kv_rows, kvw), BF16),
                 jax.ShapeDtypeStruct((kvw, kv_rows), BF16)]
    if latent:
        in_specs += [pl.BlockSpec((TM, LANES), pos_map)] * 2
        in_specs += [pl.BlockSpec((None, None, PAST_LEN, kvw), lambda b, t: (b, idx, 0, 0))] * 2
        args += list(tables) + list(cache)
    else:
        out_specs += [pl.BlockSpec((TM, kvw), kv_map)] * 2
        out_shape += [jax.ShapeDtypeStruct((n, kvw), F32)] * 2
    return pl.pallas_call(
        functools.partial(_qkv_b_kernel, latent=latent),
        grid=grid,
        in_specs=in_specs,
        out_specs=out_specs,
        out_shape=out_shape,
        compiler_params=_params(len(grid)),
        name=f"qkv_b_{'lat' if latent else 'ctx'}",
    )(*args)


def _attn_b_kernel(q_ref, k_ref, v_ref, x_ref, mod_ref, wout_ref, o_ref, s_buf, p_buf, acc_ref, *, tq, tk):
    group = HQ_B // KV_B
    nq = group * tq
    nblk = k_ref.shape[0] // tk
    row = jnp.zeros((1, nq), F32)

    def chain(kh):
        cols = slice(kh * HD_B, (kh + 1) * HD_B)
        q_fn = lambda: q_ref[kh * group:(kh + 1) * group].reshape(nq, HD_B)
        blocks = [(lambda j=j: k_ref[j * tk:(j + 1) * tk, cols], lambda j=j: v_ref[cols, j * tk:(j + 1) * tk], None)
                  for j in range(nblk)]
        return q_fn, blocks, (row + NEG_INF, row)

    outs = _attention_pipeline([chain(kh) for kh in range(KV_B)], _block_major([nblk] * KV_B),
                               s_buf, p_buf, acc_ref)
    heads = [outs[kh][:, g * tq:(g + 1) * tq].T for kh in range(KV_B) for g in range(group)]
    attn = jnp.concatenate(heads, axis=1).astype(BF16)
    o_ref[...] = x_ref[...] + mod_ref[5:6, :] * _dot(attn, wout_ref[...])


def _attn_geometry(latent):
    if latent:
        return TQ_LAT, DEC_BATCH, DEC_SEQ // TQ_LAT, N_CTX // TQ_LAT
    return SEQ, BATCH, 1, 0


def _attn_b(x, q, k, v, mods, w_out, l, idx, latent):
    kvw = KV_B * HD_B
    tq, nb, nqb, x0 = _attn_geometry(latent)
    skv = SKV_B_LAT if latent else SEQ
    tk = min(ATTN_TK, skv)
    nq = (HQ_B // KV_B) * tq
    per_group = GROUP_TOK // tq
    x_spec = pl.BlockSpec((tq, D_MODEL), lambda b, i: (x0 + b * nqb + i, 0))
    in_specs = [
        pl.BlockSpec((HQ_B, tq, HD_B), lambda b, i: (0, b * nqb + i, 0)),
        pl.BlockSpec((skv, kvw), lambda b, i: (b, 0)),
        pl.BlockSpec((kvw, skv), lambda b, i: (0, b)),
        x_spec,
        pl.BlockSpec((None, None, N_MOD, D_MODEL), lambda b, i: (l, (x0 + b * nqb + i) // per_group, 0, 0)),
        _resident((None, HQ_B * HD_B, D_MODEL), lambda b, i: (idx, 0, 0)),
    ]
    return pl.pallas_call(
        functools.partial(_attn_b_kernel, tq=tq, tk=tk),
        grid=(nb, nqb),
        in_specs=in_specs,
        out_specs=x_spec,
        out_shape=jax.ShapeDtypeStruct((N_TOK, D_MODEL), F32),
        scratch_shapes=_attention_scratch(KV_B, tk, nq, HD_B),
        input_output_aliases={3: 0},
        compiler_params=_params(2),
        name=f"attn_b_{'lat' if latent else 'ctx'}",
    )(q, k, v, x, mods, w_out)


QKV_C = (HQ_C + 2 * KV_C) * HD_C
Q_SLABS_C = HQ_C * HD_C // LANES
KV_PAD_C = 4 * LANES
SKV_C_LAT = PAST_LEN + DEC_SEQ


def _pad_variants(y):
    lane = lax.broadcasted_iota(jnp.int32, y.shape, 1)
    low = lane < HD_C
    swapped = pltpu.roll(y, HD_C, 1)
    zero = jnp.zeros_like(y)
    return jnp.concatenate([jnp.where(low, y, zero), jnp.where(low, zero, swapped),
                            jnp.where(low, swapped, zero), jnp.where(low, zero, y)], axis=1)


def _qkv_c_kernel(x_ref, mod_ref, g_ref, w_ref, *rest, latent):
    if latent:
        cos_ref, sin_ref, kc_ref, vc_ref, q_ref, k_ref, v_ref = rest
    else:
        q_ref, k_ref, v_ref, kf_ref, vf_ref = rest

    def project():
        h = _norm_mod(x_ref[...], g_ref[1:2, :], mod_ref[3:4, :], mod_ref[4:5, :]).astype(BF16)
        qkv = _dot(h, w_ref[...])
        scale = HD_C ** -0.5 * LOG2E
        for j in range(Q_SLABS_C):
            y = qkv[:, j * LANES:(j + 1) * LANES]
            if latent:
                y = _rope(y, cos_ref[...], sin_ref[...], HD_C // 4)
            q_ref[j] = (y * scale).astype(BF16)
        k = qkv[:, D_MODEL:D_MODEL + LANES]
        v = qkv[:, D_MODEL + LANES:]
        if latent:
            k = _rope(k, cos_ref[...], sin_ref[...], HD_C // 4)
        else:
            kf_ref[...] = k
            vf_ref[...] = v
        k_ref[...] = _pad_variants(k).astype(BF16)
        v_ref[...] = _pad_variants(v).T.astype(BF16)

    if latent:
        t = pl.program_id(1)

        @pl.when(t == 0)
        def _():
            k_ref[...] = _pad_variants(kc_ref[...]).astype(BF16)
            v_ref[...] = _pad_variants(vc_ref[...]).T.astype(BF16)

        pl.when(t > 0)(project)
    else:
        project()


def _qkv_c(x, mods, norm_g, w_qkv, tables, cache, l, idx, latent):
    grid, n, kv_rows, x_map, mod_map, q_map, kv_map, vt_map, pos_map = _qkv_specs(latent, l)
    in_specs = [
        pl.BlockSpec((TM, D_MODEL), x_map),
        pl.BlockSpec((None, None, N_MOD, D_MODEL), mod_map),
        pl.BlockSpec((None, 3, D_MODEL), lambda *_: (l, 0, 0)),
        _resident((None, D_MODEL, QKV_C), lambda *_: (idx, 0, 0)),
    ]
    args = [x, mods, norm_g, w_qkv]
    out_specs = [
        pl.BlockSpec((Q_SLABS_C, TM, LANES), q_map),
        pl.BlockSpec((TM, KV_PAD_C), kv_map),
        pl.BlockSpec((KV_PAD_C, TM), vt_map),
    ]
    out_shape = [jax.ShapeDtypeStruct((Q_SLABS_C, n, LANES), BF16), jax.ShapeDtypeStruct((kv_rows, KV_PAD_C), BF16),
                 jax.ShapeDtypeStruct((KV_PAD_C, kv_rows), BF16)]
    if latent:
        in_specs += [pl.BlockSpec((TM, LANES), pos_map)] * 2
        in_specs += [pl.BlockSpec((None, None, PAST_LEN, LANES), lambda b, t: (b, idx, 0, 0))] * 2
        args += list(tables) + list(cache)
    else:
        out_specs += [pl.BlockSpec((TM, LANES), kv_map)] * 2
        out_shape += [jax.ShapeDtypeStruct((n, LANES), F32)] * 2
    return pl.pallas_call(
        functools.partial(_qkv_c_kernel, latent=latent),
        grid=grid,
        in_specs=in_specs,
        out_specs=out_specs,
        out_shape=out_shape,
        compiler_params=_params(len(grid)),
        name=f"qkv_c_{'lat' if latent else 'ctx'}",
    )(*args)


def _attn_c_kernel(sink_ref, q_ref, k_ref, v_ref, x_ref, mod_ref, wout_ref, o_ref, s_buf, p_buf, acc_ref,
                   *, tq, latent):
    group = HQ_C // KV_C
    pairs = group // 2
    nq = pairs * tq
    if latent:
        i = pl.program_id(1)
        start = pl.multiple_of(jnp.clip(i * tq - WINDOW, 0, DEC_SEQ - BAND), WINDOW)
        band = pl.ds(pl.multiple_of(PAST_LEN + start, WINDOW), BAND)
        qpos = i * tq + (lax.broadcasted_iota(jnp.int32, (BAND, nq), 1) & (tq - 1))
        kpos = start + lax.broadcasted_iota(jnp.int32, (BAND, nq), 0)
        near = jnp.abs(qpos - kpos) <= WINDOW
        spans = [(slice(0, PAST_LEN), None), (band, near)]
    else:
        spans = [(slice(None), None)]

    def chain(kh, par):
        cols = slice((2 * kh + par) * LANES, (2 * kh + par + 1) * LANES)
        q_fn = lambda: q_ref[kh * pairs:(kh + 1) * pairs].reshape(nq, LANES)
        blocks = [(lambda r=r: k_ref[r, cols], lambda r=r: v_ref[cols, r], mask) for r, mask in spans]
        sink = jnp.concatenate([jnp.full((1, tq), sink_ref[kh * group + 2 * j + par] * LOG2E, F32)
                                for j in range(pairs)], axis=1)
        return q_fn, blocks, (sink, jnp.ones_like(sink))

    chains = [chain(kh, par) for kh in range(KV_C) for par in range(2)]
    outs = _attention_pipeline(chains, _block_major([len(spans)] * len(chains)), s_buf, p_buf, acc_ref)
    slabs = []
    for kh in range(KV_C):
        o_t = outs[2 * kh] + outs[2 * kh + 1]
        slabs += [o_t[:, j * tq:(j + 1) * tq].T for j in range(pairs)]
    attn = jnp.concatenate(slabs, axis=1).astype(BF16)
    o_ref[...] = x_ref[...] + mod_ref[5:6, :] * _dot(attn, wout_ref[...])


def _attn_c(x, q, k, v, sink, mods, w_out, l, idx, latent):
    tq, nb, nqb, x0 = _attn_geometry(latent)
    skv = SKV_C_LAT if latent else SEQ
    nq = (HQ_C // KV_C // 2) * tq
    per_group = GROUP_TOK // tq
    x_spec = pl.BlockSpec((tq, D_MODEL), lambda b, i: (x0 + b * nqb + i, 0))
    in_specs = [
        pl.BlockSpec(memory_space=pltpu.SMEM),
        pl.BlockSpec((Q_SLABS_C, tq, LANES), lambda b, i: (0, b * nqb + i, 0)),
        pl.BlockSpec((skv, KV_PAD_C), lambda b, i: (b, 0)),
        pl.BlockSpec((KV_PAD_C, skv), lambda b, i: (0, b)),
        x_spec,
        pl.BlockSpec((None, None, N_MOD, D_MODEL), lambda b, i: (l, (x0 + b * nqb + i) // per_group, 0, 0)),
        _resident((None, HQ_C * HD_C, D_MODEL), lambda b, i: (idx, 0, 0)),
    ]
    return pl.pallas_call(
        functools.partial(_attn_c_kernel, tq=tq, latent=latent),
        grid=(nb, nqb),
        in_specs=in_specs,
        out_specs=x_spec,
        out_shape=jax.ShapeDtypeStruct((N_TOK, D_MODEL), F32),
        scratch_shapes=_attention_scratch(2 * KV_C, PAST_LEN if latent else SEQ, nq, LANES),
        input_output_aliases={4: 0},
        compiler_params=_params(2),
        name=f"attn_c_{'lat' if latent else 'ctx'}",
    )(sink, q, k, v, x, mods, w_out)


def kernel(x_prompt, x_sample, cache_b_k, cache_b_v, cache_c_k, cache_c_v, c, c_ctx, w_mod, b_mod, norm_g,
           ffn_w_gate, ffn_w_up, ffn_w_down, a_w_in, a_norm_g, a_w_s, a_b_s, a_w_out, b_w_qkv, b_q_g, b_k_g,
           b_w_out, c_w_qkv, c_sink, c_w_out, final_g):
    x = jnp.concatenate([x_prompt.reshape(N_CTX, D_MODEL), x_sample.reshape(N_LAT, D_MODEL)], axis=0)
    cond = jnp.concatenate([c_ctx[None, :], c, jnp.zeros((MOD_ROWS - 1 - DEC_BATCH, D_MODEL), F32)], axis=0)
    mods = _ada_params(cond, w_mod, b_mod)

    wg, wu, wd = ffn_w_gate.astype(BF16), ffn_w_up.astype(BF16), ffn_w_down.astype(BF16)
    a_w_in_h, a_w_s_h, a_w_out_h = a_w_in.astype(BF16), a_w_s.astype(BF16), a_w_out.astype(BF16)
    a_b_s_t = jnp.swapaxes(a_b_s, 1, 2)
    a_norm_g3 = a_norm_g[:, None, :]
    b_w_qkv_h, b_w_out_h = b_w_qkv.astype(BF16), b_w_out.astype(BF16)
    c_w_qkv_h, c_w_out_h = c_w_qkv.astype(BF16), c_w_out.astype(BF16)
    tables_b = _rope_tables(HD_B)
    tables_c = _rope_tables(HD_C)
    cache_b = (cache_b_k.reshape(DEC_BATCH, -1, PAST_LEN, KV_B * HD_B),
               cache_b_v.reshape(DEC_BATCH, -1, PAST_LEN, KV_B * HD_B))
    cache_c = (cache_c_k.reshape(DEC_BATCH, -1, PAST_LEN, KV_C * HD_C),
               cache_c_v.reshape(DEC_BATCH, -1, PAST_LEN, KV_C * HD_C))

    new_kv = {1: [], 2: []}
    for l in range(DEPTH):
        kind, idx = l % 3, l // 3
        x = _ffn(x, mods, norm_g, wg, wu, wd, l, 0)
        if kind == 0:
            x = _mixer_a(x, mods, norm_g, a_w_in_h, a_norm_g3, a_w_s_h, a_b_s_t, a_w_out_h, l, idx)
        elif kind == 1:
            q_g, k_g = b_q_g[:, None, :], b_k_g[:, None, :]
            qc, kc, vc, kf, vf = _qkv_b(x, mods, norm_g, b_w_qkv_h, q_g, k_g, None, None, l, idx, False)
            ql, kl, vl = _qkv_b(x, mods, norm_g, b_w_qkv_h, q_g, k_g, tables_b, cache_b, l, idx, True)
            new_kv[1].append((kf.reshape(BATCH, SEQ, KV_B, HD_B), vf.reshape(BATCH, SEQ, KV_B, HD_B)))
            x = _attn_b(x, qc, kc, vc, mods, b_w_out_h, l, idx, False)
            x = _attn_b(x, ql, kl, vl, mods, b_w_out_h, l, idx, True)
        else:
            qc, kc, vc, kf, vf = _qkv_c(x, mods, norm_g, c_w_qkv_h, None, None, l, idx, False)
            ql, kl, vl = _qkv_c(x, mods, norm_g, c_w_qkv_h, tables_c, cache_c, l, idx, True)
            new_kv[2].append((kf.reshape(BATCH, SEQ, KV_C, HD_C), vf.reshape(BATCH, SEQ, KV_C, HD_C)))
            sink = c_sink[idx]
            x = _attn_c(x, qc, kc, vc, sink, mods, c_w_out_h, l, idx, False)
            x = _attn_c(x, ql, kl, vl, sink, mods, c_w_out_h, l, idx, True)
        if l < DEPTH - 1:
            x = _ffn(x, mods, norm_g, wg, wu, wd, l, 1)
    l = DEPTH - 1
    y_prompt = _ffn(x, mods, norm_g, wg, wu, wd, l, 1, final_g=final_g, rows=(0, N_CTX))
    y_sample = _ffn(x, mods, norm_g, wg, wu, wd, l, 1, final_g=final_g, rows=(N_CTX, N_TOK))
    new_b_k = jnp.stack([kv[0] for kv in new_kv[1]], axis=1)
    new_b_v = jnp.stack([kv[1] for kv in new_kv[1]], axis=1)
    new_c_k = jnp.stack([kv[0] for kv in new_kv[2]], axis=1)
    new_c_v = jnp.stack([kv[1] for kv in new_kv[2]], axis=1)
    return (y_prompt.reshape(BATCH, SEQ, D_MODEL), y_sample.reshape(DEC_BATCH, DEC_SEQ, D_MODEL),
            new_b_k, new_b_v, new_c_k, new_c_v)
```

```python
import functools

import jax
import jax.numpy as jnp
from jax import lax
from jax.experimental import pallas as pl
from jax.experimental.pallas import tpu as pltpu

F32 = jnp.float32
BF16 = jnp.bfloat16

D_MODEL = 1024
BATCH = 16
SEQ = 256
DEPTH = 4
DEC_BATCH = 4
DEC_SEQ = 4096
PAST_LEN = 512
GRID_W = 64
N_MOD = 9
D_FF = 2816
CHUNK = 128
D_A = 2 * D_MODEL
G_A = 8
C_A = D_A // G_A
HQ_B, KV_B, HD_B = 8, 2, 128
HQ_C, KV_C, HD_C = 16, 2, 64
WINDOW = 128
ROPE_THETA = 10000.0
EPS = 1e-6
NEG_INF = -1e30
LOG2E = 1.4426950408889634

N_CTX = BATCH * SEQ
N_LAT = DEC_BATCH * DEC_SEQ
N_TOK = N_CTX + N_LAT
GROUP_TOK = 4096
MOD_ROWS = 8
LANES = 128

TM = 512
TQ_LAT = 128
ATTN_TK = 512
BAND = 3 * WINDOW
VMEM_LIMIT = 56 * 1024 * 1024


def _params(n_grid):
    return pltpu.CompilerParams(dimension_semantics=("arbitrary",) * n_grid,
                                vmem_limit_bytes=VMEM_LIMIT)


def _resident(block_shape, index_map):
    return pl.BlockSpec(block_shape, index_map, pipeline_mode=pl.Buffered(1))


def _dot(a, b):
    return jnp.dot(a, b, preferred_element_type=F32)


def _dot_nt(a, b):
    return lax.dot_general(a, b, (((1,), (1,)), ((), ())), preferred_element_type=F32)


def _dot_tn(a, b):
    return lax.dot_general(a, b, (((0,), (0,)), ((), ())), preferred_element_type=F32)


def _rms(x):
    return x * lax.rsqrt(jnp.mean(x * x, axis=-1, keepdims=True) + EPS)


def _norm_mod(x, g, shift, scale):
    return (_rms(x) * g) * (1.0 + scale) + shift


ADA_TN = 2304


def _ada_kernel(cond_ref, w_ref, b_ref, o_ref):
    s = jax.nn.silu(cond_ref[...]).astype(BF16)
    o_ref[...] = _dot(s, w_ref[...].astype(BF16)) + b_ref[...]


def _ada_params(cond, w_mod, b_mod):
    n = N_MOD * D_MODEL
    out = pl.pallas_call(
        _ada_kernel,
        grid=(DEPTH, n // ADA_TN),
        in_specs=[
            pl.BlockSpec((MOD_ROWS, D_MODEL), lambda l, j: (0, 0)),
            pl.BlockSpec((None, D_MODEL, ADA_TN), lambda l, j: (l, 0, j)),
            pl.BlockSpec((None, 1, ADA_TN), lambda l, j: (l, 0, j)),
        ],
        out_specs=pl.BlockSpec((None, MOD_ROWS, ADA_TN), lambda l, j: (l, 0, j)),
        out_shape=jax.ShapeDtypeStruct((DEPTH, MOD_ROWS, n), F32),
        compiler_params=_params(2),
        name="ada_params",
    )(cond, w_mod, b_mod.reshape(DEPTH, 1, n))
    return out.reshape(DEPTH, MOD_ROWS, N_MOD, D_MODEL)


def _ffn_kernel(x_ref, mod_ref, g_ref, wg_ref, wu_ref, wd_ref, *rest, k, gi, final):
    x = x_ref[...]
    h = _norm_mod(x, g_ref[gi:gi + 1, :], mod_ref[k:k + 1, :], mod_ref[k + 1:k + 2, :]).astype(BF16)
    a = (jax.nn.silu(_dot(h, wg_ref[...])) * _dot(h, wu_ref[...])).astype(BF16)
    y = x + (0.5 * mod_ref[k + 2:k + 3, :]) * _dot(a, wd_ref[...])
    if final:
        fg_ref, o_ref = rest
        y = _rms(y) * fg_ref[...]
    else:
        (o_ref,) = rest
    o_ref[...] = y


def _ffn(x, mods, norm_g, wg, wu, wd, l, half, *, final_g=None, rows=None):
    k, gi = (0, 0) if half == 0 else (6, 2)
    r0, r1 = (0, N_TOK) if rows is None else rows
    t0, nt = r0 // TM, (r1 - r0) // TM
    per_group = GROUP_TOK // TM
    final = final_g is not None
    in_specs = [
        pl.BlockSpec((TM, D_MODEL), lambda i: (t0 + i, 0)),
        pl.BlockSpec((None, None, N_MOD, D_MODEL), lambda i: (l, (t0 + i) // per_group, 0, 0)),
        pl.BlockSpec((None, 3, D_MODEL), lambda i: (l, 0, 0)),
        _resident((None, None, D_MODEL, D_FF), lambda i: (l, half, 0, 0)),
        _resident((None, None, D_MODEL, D_FF), lambda i: (l, half, 0, 0)),
        _resident((None, None, D_FF, D_MODEL), lambda i: (l, half, 0, 0)),
    ]
    args = [x, mods, norm_g, wg, wu, wd]
    if final:
        in_specs.append(pl.BlockSpec((1, D_MODEL), lambda i: (0, 0)))
        args.append(final_g.reshape(1, D_MODEL))
    return pl.pallas_call(
        functools.partial(_ffn_kernel, k=k, gi=gi, final=final),
        grid=(nt,),
        in_specs=in_specs,
        out_specs=pl.BlockSpec((TM, D_MODEL), lambda i: (i, 0)),
        out_shape=jax.ShapeDtypeStruct((nt * TM, D_MODEL), F32),
        compiler_params=_params(1),
        name=f"ffn_l{l}_h{half}" + ("_final" if final else ""),
    )(*args)


def _sgu_kernel(x_ref, mod_ref, g_ref, win_ref, ng_ref, ws_ref, bs_ref, wout_ref, o_ref, gated_ref):
    x = x_ref[...]
    h = _norm_mod(x, g_ref[1:2, :], mod_ref[3:4, :], mod_ref[4:5, :]).astype(BF16)
    uv = _dot(h, win_ref[...])
    v = (_rms(uv[:, D_A:]) * ng_ref[...]).astype(BF16)
    for c in range(TM // CHUNK):
        rows = slice(c * CHUNK, (c + 1) * CHUNK)
        for g in range(G_A):
            cols = slice(g * C_A, (g + 1) * C_A)
            mixed = _dot(ws_ref[g], v[rows, cols]) + bs_ref[:, g:g + 1]
            gated_ref[rows, cols] = (uv[rows, cols] * mixed).astype(BF16)
    o_ref[...] = x + mod_ref[5:6, :] * _dot(gated_ref[...], wout_ref[...])


def _mixer_a(x, mods, norm_g, w_in, a_norm_g, w_s, b_s_t, w_out, l, idx):
    per_group = GROUP_TOK // TM
    return pl.pallas_call(
        _sgu_kernel,
        grid=(N_TOK // TM,),
        in_specs=[
            pl.BlockSpec((TM, D_MODEL), lambda i: (i, 0)),
            pl.BlockSpec((None, None, N_MOD, D_MODEL), lambda i: (l, i // per_group, 0, 0)),
            pl.BlockSpec((None, 3, D_MODEL), lambda i: (l, 0, 0)),
            _resident((None, D_MODEL, 2 * D_A), lambda i: (idx, 0, 0)),
            pl.BlockSpec((None, 1, D_A), lambda i: (idx, 0, 0)),
            pl.BlockSpec((None, G_A, CHUNK, CHUNK), lambda i: (idx, 0, 0, 0)),
            pl.BlockSpec((None, CHUNK, G_A), lambda i: (idx, 0, 0)),
            _resident((None, D_A, D_MODEL), lambda i: (idx, 0, 0)),
        ],
        out_specs=pl.BlockSpec((TM, D_MODEL), lambda i: (i, 0)),
        out_shape=jax.ShapeDtypeStruct((N_TOK, D_MODEL), F32),
        scratch_shapes=[pltpu.VMEM((TM, D_A), BF16)],
        compiler_params=_params(1),
        name=f"mixer_a_l{l}",
    )(x, mods, norm_g, w_in, a_norm_g, w_s, b_s_t, w_out)


def _rope_t(y, cos, sin, quarter):
    pieces = []
    for s in range(0, y.shape[0], 2 * quarter):
        pieces += [y[s + quarter:s + 2 * quarter], y[s:s + quarter]]
    return y * cos + jnp.concatenate(pieces, axis=0) * sin


def _rope_tables(hd):
    rows = DEC_SEQ // GRID_W
    row = jnp.repeat(jnp.arange(rows), GRID_W).astype(F32)
    col = (jnp.arange(DEC_SEQ) % GRID_W).astype(F32)
    quarter = hd // 4
    inv = ROPE_THETA ** (-jnp.arange(quarter, dtype=F32) / quarter)
    ang_r = row[:, None] * inv[None, :]
    ang_c = col[:, None] * inv[None, :]
    cos = jnp.concatenate([jnp.cos(ang_r)] * 2 + [jnp.cos(ang_c)] * 2, axis=-1)
    sin = jnp.concatenate([-jnp.sin(ang_r), jnp.sin(ang_r), -jnp.sin(ang_c), jnp.sin(ang_c)], axis=-1)
    reps = LANES // hd
    return jnp.tile(cos, (1, reps)).T, jnp.tile(sin, (1, reps)).T


def _attention_pipeline(chains, order, s_buf, p_buf, acc_ref):
    n = len(order)
    state = [ch[2] for ch in chains]
    rows_of, alpha_of = [None] * n, [None] * n
    left = [len(ch[1]) for ch in chains]
    outs = [None] * len(chains)

    def scores(t):
        c, b = order[t]
        k_fn, _, mask = chains[c][1][b]
        st = _dot(k_fn(), chains[c][0]())
        if mask is not None:
            st = jnp.where(mask, st, NEG_INF)
        rows_of[t] = st.shape[0]
        s_buf[t % 2, :st.shape[0]] = st

    def softmax(t):
        c, _ = order[t]
        m, l = state[c]
        st = s_buf[t % 2, :rows_of[t]]
        m_new = jnp.maximum(m, jnp.max(st, axis=0, keepdims=True))
        alpha_of[t] = jnp.exp2(m - m_new)
        p = jnp.exp2(st - m_new)
        p_buf[t % 2, :rows_of[t]] = p.astype(BF16)
        state[c] = (m_new, alpha_of[t] * l + jnp.sum(p, axis=0, keepdims=True))

    def weighted_values(t):
        c, b = order[t]
        pv = _dot(chains[c][1][b][1](), p_buf[t % 2, :rows_of[t]])
        acc_ref[c] = alpha_of[t] * acc_ref[c] + pv
        left[c] -= 1
        if left[c] == 0:
            outs[c] = acc_ref[c] / state[c][1]

    acc_ref[...] = jnp.zeros_like(acc_ref)
    for t in range(-1, n + 1):
        if t + 1 < n:
            scores(t + 1)
        if 0 <= t < n:
            softmax(t)
        if 0 <= t - 1:
            weighted_values(t - 1)
    return outs


def _attention_scratch(n_chains, n_max, nq, hd):
    return [pltpu.VMEM((2, n_max, nq), F32), pltpu.VMEM((2, n_max, nq), BF16),
            pltpu.VMEM((n_chains, hd, nq), F32)]


def _block_major(chains_blocks):
    depth = max(chains_blocks)
    return [(c, b) for b in range(depth) for c, nb in enumerate(chains_blocks) if b < nb]


QKV_B = (HQ_B + 2 * KV_B) * HD_B
SKV_B_LAT = PAST_LEN + DEC_SEQ


def _qkv_b_kernel(x_ref, mod_ref, g_ref, wt_ref, qg_ref, kg_ref, *rest, latent):
    if latent:
        cos_ref, sin_ref, kc_ref, vc_ref, q_ref, k_ref, v_ref = rest
    else:
        q_ref, k_ref, v_ref, kf_ref, vf_ref = rest

    def project():
        h = _norm_mod(x_ref[...], g_ref[1:2, :], mod_ref[3:4, :], mod_ref[4:5, :]).astype(BF16)
        qkv_t = _dot_nt(wt_ref[...], h)
        scale = HD_B ** -0.5 * LOG2E
        qg = jnp.tile(qg_ref[...], (1, TM // LANES))
        kg = jnp.tile(kg_ref[...], (1, TM // LANES))
        for j in range(HQ_B + KV_B):
            is_q = j < HQ_B
            y = qkv_t[j * HD_B:(j + 1) * HD_B, :]
            y = y * lax.rsqrt(jnp.mean(y * y, axis=0, keepdims=True) + EPS) * (qg if is_q else kg)
            kcols = slice((j - HQ_B) * HD_B, (j - HQ_B + 1) * HD_B)
            if not latent and not is_q:
                kf_ref[:, kcols] = y.T
            if latent:
                y = _rope_t(y, cos_ref[...], sin_ref[...], HD_B // 4)
            if is_q:
                q_ref[j] = (y * scale).astype(BF16)
            else:
                k_ref[:, kcols] = y.T.astype(BF16)
        v_t = qkv_t[(HQ_B + KV_B) * HD_B:, :]
        v_ref[...] = v_t.astype(BF16)
        if not latent:
            vf_ref[...] = v_t.T

    if latent:
        t = pl.program_id(1)

        @pl.when(t == 0)
        def _():
            k_ref[...] = kc_ref[...].astype(BF16)
            v_ref[...] = vc_ref[...].T.astype(BF16)

        pl.when(t > 0)(project)
    else:
        project()


def _qkv_specs(latent, l):
    per_batch = DEC_SEQ // TM
    if latent:
        assert PAST_LEN == TM
        grid = (DEC_BATCH, 1 + per_batch)

        def tile(b, t):
            return b * per_batch + jnp.maximum(t - 1, 0)

        x_map = lambda b, t: (N_CTX // TM + tile(b, t), 0)
        mod_map = lambda b, t: (l, 1 + b, 0, 0)
        q_map = lambda b, t: (0, 0, tile(b, t))
        kv_map = lambda b, t: (b * (1 + per_batch) + t, 0)
        vt_map = lambda b, t: (0, b * (1 + per_batch) + t)
        pos_map = lambda b, t: (0, jnp.maximum(t - 1, 0))
        return grid, N_LAT, DEC_BATCH * SKV_B_LAT, x_map, mod_map, q_map, kv_map, vt_map, pos_map
    x_map = lambda i: (i, 0)
    return ((N_CTX // TM,), N_CTX, N_CTX, x_map, lambda i: (l, 0, 0, 0), lambda i: (0, 0, i), x_map,
            lambda i: (0, i), None)


def _qkv_b(x, mods, norm_g, w_qkv, q_g, k_g, tables, cache, l, idx, latent):
    kvw = KV_B * HD_B
    grid, n, kv_rows, x_map, mod_map, q_map, kv_map, vt_map, pos_map = _qkv_specs(latent, l)
    const = lambda *_: (idx, 0, 0)
    in_specs = [
        pl.BlockSpec((TM, D_MODEL), x_map),
        pl.BlockSpec((None, None, N_MOD, D_MODEL), mod_map),
        pl.BlockSpec((None, 3, D_MODEL), lambda *_: (l, 0, 0)),
        _resident((None, QKV_B, D_MODEL), const),
        pl.BlockSpec((None, HD_B, LANES), const),
        pl.BlockSpec((None, HD_B, LANES), const),
    ]
    args = [x, mods, norm_g, w_qkv, q_g, k_g]
    out_specs = [
        pl.BlockSpec((HQ_B, HD_B, TM), q_map),
        pl.BlockSpec((TM, kvw), kv_map),
        pl.BlockSpec((kvw, TM), vt_map),
    ]
    out_shape = [jax.ShapeDtypeStruct((HQ_B, HD_B, n), BF16), jax.ShapeDtypeStruct((kv_rows, kvw), BF16),
                 jax.ShapeDtypeStruct((kvw, kv_rows), BF16)]
    if latent:
        in_specs += [pl.BlockSpec((LANES, TM), pos_map)] * 2
        in_specs += [pl.BlockSpec((None, None, PAST_LEN, kvw), lambda b, t: (b, idx, 0, 0))] * 2
        args += list(tables) + list(cache)
    else:
        out_specs += [pl.BlockSpec((TM, kvw), kv_map)] * 2
        out_shape += [jax.ShapeDtypeStruct((n, kvw), F32)] * 2
    return pl.pallas_call(
        functools.partial(_qkv_b_kernel, latent=latent),
        grid=grid,
        in_specs=in_specs,
        out_specs=out_specs,
        out_shape=out_shape,
        compiler_params=_params(len(grid)),
        name=f"qkv_b_{'lat' if latent else 'ctx'}",
    )(*args)


def _attn_b_kernel(q_ref, k_ref, v_ref, x_ref, mod_ref, wout_ref, o_ref, s_buf, p_buf, acc_ref, *, tq, tk):
    group = HQ_B // KV_B
    nq = group * tq
    nblk = k_ref.shape[0] // tk
    row = jnp.zeros((1, nq), F32)

    def chain(kh):
        cols = slice(kh * HD_B, (kh + 1) * HD_B)
        q_fn = lambda: jnp.concatenate([q_ref[kh * group + g] for g in range(group)], axis=1)
        blocks = [(lambda j=j: k_ref[j * tk:(j + 1) * tk, cols], lambda j=j: v_ref[cols, j * tk:(j + 1) * tk], None)
                  for j in range(nblk)]
        return q_fn, blocks, (row + NEG_INF, row)

    outs = _attention_pipeline([chain(kh) for kh in range(KV_B)], _block_major([nblk] * KV_B),
                               s_buf, p_buf, acc_ref)
    heads = [outs[kh][:, g * tq:(g + 1) * tq].T for kh in range(KV_B) for g in range(group)]
    attn = jnp.concatenate(heads, axis=1).astype(BF16)
    o_ref[...] = x_ref[...] + mod_ref[5:6, :] * _dot(attn, wout_ref[...])


def _attn_geometry(latent):
    if latent:
        return TQ_LAT, DEC_BATCH, DEC_SEQ // TQ_LAT, N_CTX // TQ_LAT
    return SEQ, BATCH, 1, 0


def _attn_b(x, q, k, v, mods, w_out, l, idx, latent):
    kvw = KV_B * HD_B
    tq, nb, nqb, x0 = _attn_geometry(latent)
    skv = SKV_B_LAT if latent else SEQ
    tk = min(ATTN_TK, skv)
    nq = (HQ_B // KV_B) * tq
    per_group = GROUP_TOK // tq
    x_spec = pl.BlockSpec((tq, D_MODEL), lambda b, i: (x0 + b * nqb + i, 0))
    in_specs = [
        pl.BlockSpec((HQ_B, HD_B, tq), lambda b, i: (0, 0, b * nqb + i)),
        pl.BlockSpec((skv, kvw), lambda b, i: (b, 0)),
        pl.BlockSpec((kvw, skv), lambda b, i: (0, b)),
        x_spec,
        pl.BlockSpec((None, None, N_MOD, D_MODEL), lambda b, i: (l, (x0 + b * nqb + i) // per_group, 0, 0)),
        _resident((None, HQ_B * HD_B, D_MODEL), lambda b, i: (idx, 0, 0)),
    ]
    return pl.pallas_call(
        functools.partial(_attn_b_kernel, tq=tq, tk=tk),
        grid=(nb, nqb),
        in_specs=in_specs,
        out_specs=x_spec,
        out_shape=jax.ShapeDtypeStruct((N_TOK, D_MODEL), F32),
        scratch_shapes=_attention_scratch(KV_B, tk, nq, HD_B),
        input_output_aliases={3: 0},
        compiler_params=_params(2),
        name=f"attn_b_{'lat' if latent else 'ctx'}",
    )(q, k, v, x, mods, w_out)


QKV_C = (HQ_C + 2 * KV_C) * HD_C
Q_SLABS_C = HQ_C * HD_C // LANES
KV_PAD_C = 4 * LANES
SKV_C_LAT = PAST_LEN + DEC_SEQ


def _pad_variants(y):
    lane = lax.broadcasted_iota(jnp.int32, y.shape, 1)
    low = lane < HD_C
    swapped = pltpu.roll(y, HD_C, 1)
    zero = jnp.zeros_like(y)
    return jnp.concatenate([jnp.where(low, y, zero), jnp.where(low, zero, swapped),
                            jnp.where(low, swapped, zero), jnp.where(low, zero, y)], axis=1)


def _pad_variants_t(y_t):
    h0, h1 = y_t[:HD_C], y_t[HD_C:]
    zero = jnp.zeros_like(h0)
    return jnp.concatenate([h0, zero, zero, h0, h1, zero, zero, h1], axis=0)


def _qkv_c_kernel(x_ref, mod_ref, g_ref, wt_ref, *rest, latent):
    if latent:
        cos_ref, sin_ref, kc_ref, vc_ref, q_ref, k_ref, v_ref = rest
    else:
        q_ref, k_ref, v_ref, kf_ref, vf_ref = rest

    def project():
        h = _norm_mod(x_ref[...], g_ref[1:2, :], mod_ref[3:4, :], mod_ref[4:5, :]).astype(BF16)
        qkv_t = _dot_nt(wt_ref[...], h)
        scale = HD_C ** -0.5 * LOG2E
        for j in range(Q_SLABS_C):
            y = qkv_t[j * LANES:(j + 1) * LANES, :]
            if latent:
                y = _rope_t(y, cos_ref[...], sin_ref[...], HD_C // 4)
            q_ref[j] = (y * scale).astype(BF16)
        k_t = qkv_t[D_MODEL:D_MODEL + LANES, :]
        v_t = qkv_t[D_MODEL + LANES:, :]
        if latent:
            k_t = _rope_t(k_t, cos_ref[...], sin_ref[...], HD_C // 4)
        k = k_t.T
        if not latent:
            kf_ref[...] = k
            vf_ref[...] = v_t.T
        k_ref[...] = _pad_variants(k).astype(BF16)
        v_ref[...] = _pad_variants_t(v_t).astype(BF16)

    if latent:
        t = pl.program_id(1)

        @pl.when(t == 0)
        def _():
            k_ref[...] = _pad_variants(kc_ref[...]).astype(BF16)
            v_ref[...] = _pad_variants_t(vc_ref[...].T).astype(BF16)

        pl.when(t > 0)(project)
    else:
        project()


def _qkv_c(x, mods, norm_g, w_qkv, tables, cache, l, idx, latent):
    grid, n, kv_rows, x_map, mod_map, q_map, kv_map, vt_map, pos_map = _qkv_specs(latent, l)
    in_specs = [
        pl.BlockSpec((TM, D_MODEL), x_map),
        pl.BlockSpec((None, None, N_MOD, D_MODEL), mod_map),
        pl.BlockSpec((None, 3, D_MODEL), lambda *_: (l, 0, 0)),
        _resident((None, QKV_C, D_MODEL), lambda *_: (idx, 0, 0)),
    ]
    args = [x, mods, norm_g, w_qkv]
    out_specs = [
        pl.BlockSpec((Q_SLABS_C, LANES, TM), q_map),
        pl.BlockSpec((TM, KV_PAD_C), kv_map),
        pl.BlockSpec((KV_PAD_C, TM), vt_map),
    ]
    out_shape = [jax.ShapeDtypeStruct((Q_SLABS_C, LANES, n), BF16), jax.ShapeDtypeStruct((kv_rows, KV_PAD_C), BF16),
                 jax.ShapeDtypeStruct((KV_PAD_C, kv_rows), BF16)]
    if latent:
        in_specs += [pl.BlockSpec((LANES, TM), pos_map)] * 2
        in_specs += [pl.BlockSpec((None, None, PAST_LEN, LANES), lambda b, t: (b, idx, 0, 0))] * 2
        args += list(tables) + list(cache)
    else:
        out_specs += [pl.BlockSpec((TM, LANES), kv_map)] * 2
        out_shape += [jax.ShapeDtypeStruct((n, LANES), F32)] * 2
    return pl.pallas_call(
        functools.partial(_qkv_c_kernel, latent=latent),
        grid=grid,
        in_specs=in_specs,
        out_specs=out_specs,
        out_shape=out_shape,
        compiler_params=_params(len(grid)),
        name=f"qkv_c_{'lat' if latent else 'ctx'}",
    )(*args)


def _attn_c_kernel(sink_ref, q_ref, k_ref, v_ref, x_ref, mod_ref, wout_ref, o_ref, s_buf, p_buf, acc_ref,
                   *, tq, latent):
    group = HQ_C // KV_C
    pairs = group // 2
    nq = pairs * tq
    if latent:
        i = pl.program_id(1)
        start = pl.multiple_of(jnp.clip(i * tq - WINDOW, 0, DEC_SEQ - BAND), WINDOW)
        band = pl.ds(pl.multiple_of(PAST_LEN + start, WINDOW), BAND)
        qpos = i * tq + (lax.broadcasted_iota(jnp.int32, (BAND, nq), 1) & (tq - 1))
        kpos = start + lax.broadcasted_iota(jnp.int32, (BAND, nq), 0)
        near = jnp.abs(qpos - kpos) <= WINDOW
        spans = [(slice(0, PAST_LEN), None), (band, near)]
    else:
        spans = [(slice(None), None)]

    def chain(kh, par):
        cols = slice((2 * kh + par) * LANES, (2 * kh + par + 1) * LANES)
        q_fn = lambda: jnp.concatenate([q_ref[kh * pairs + j] for j in range(pairs)], axis=1)
        blocks = [(lambda r=r: k_ref[r, cols], lambda r=r: v_ref[cols, r], mask) for r, mask in spans]
        sink = jnp.concatenate([jnp.full((1, tq), sink_ref[kh * group + 2 * j + par] * LOG2E, F32)
                                for j in range(pairs)], axis=1)
        return q_fn, blocks, (sink, jnp.ones_like(sink))

    chains = [chain(kh, par) for kh in range(KV_C) for par in range(2)]
    outs = _attention_pipeline(chains, _block_major([len(spans)] * len(chains)), s_buf, p_buf, acc_ref)
    slabs = []
    for kh in range(KV_C):
        o_t = outs[2 * kh] + outs[2 * kh + 1]
        slabs += [o_t[:, j * tq:(j + 1) * tq].T for j in range(pairs)]
    attn = jnp.concatenate(slabs, axis=1).astype(BF16)
    o_ref[...] = x_ref[...] + mod_ref[5:6, :] * _dot(attn, wout_ref[...])


def _attn_c(x, q, k, v, sink, mods, w_out, l, idx, latent):
    tq, nb, nqb, x0 = _attn_geometry(latent)
    skv = SKV_C_LAT if latent else SEQ
    nq = (HQ_C // KV_C // 2) * tq
    per_group = GROUP_TOK // tq
    x_spec = pl.BlockSpec((tq, D_MODEL), lambda b, i: (x0 + b * nqb + i, 0))
    in_specs = [
        pl.BlockSpec(memory_space=pltpu.SMEM),
        pl.BlockSpec((Q_SLABS_C, LANES, tq), lambda b, i: (0, 0, b * nqb + i)),
        pl.BlockSpec((skv, KV_PAD_C), lambda b, i: (b, 0)),
        pl.BlockSpec((KV_PAD_C, skv), lambda b, i: (0, b)),
        x_spec,
        pl.BlockSpec((None, None, N_MOD, D_MODEL), lambda b, i: (l, (x0 + b * nqb + i) // per_group, 0, 0)),
        _resident((None, HQ_C * HD_C, D_MODEL), lambda b, i: (idx, 0, 0)),
    ]
    return pl.pallas_call(
        functools.partial(_attn_c_kernel, tq=tq, latent=latent),
        grid=(nb, nqb),
        in_specs=in_specs,
        out_specs=x_spec,
        out_shape=jax.ShapeDtypeStruct((N_TOK, D_MODEL), F32),
        scratch_shapes=_attention_scratch(2 * KV_C, PAST_LEN if latent else SEQ, nq, LANES),
        input_output_aliases={4: 0},
        compiler_params=_params(2),
        name=f"attn_c_{'lat' if latent else 'ctx'}",
    )(sink, q, k, v, x, mods, w_out)


def kernel(x_prompt, x_sample, cache_b_k, cache_b_v, cache_c_k, cache_c_v, c, c_ctx, w_mod, b_mod, norm_g,
           ffn_w_gate, ffn_w_up, ffn_w_down, a_w_in, a_norm_g, a_w_s, a_b_s, a_w_out, b_w_qkv, b_q_g, b_k_g,
           b_w_out, c_w_qkv, c_sink, c_w_out, final_g):
    x = jnp.concatenate([x_prompt.reshape(N_CTX, D_MODEL), x_sample.reshape(N_LAT, D_MODEL)], axis=0)
    cond = jnp.concatenate([c_ctx[None, :], c, jnp.zeros((MOD_ROWS - 1 - DEC_BATCH, D_MODEL), F32)], axis=0)
    mods = _ada_params(cond, w_mod, b_mod)

    wg, wu, wd = ffn_w_gate.astype(BF16), ffn_w_up.astype(BF16), ffn_w_down.astype(BF16)
    a_w_in_h, a_w_s_h, a_w_out_h = a_w_in.astype(BF16), a_w_s.astype(BF16), a_w_out.astype(BF16)
    a_b_s_t = jnp.swapaxes(a_b_s, 1, 2)
    a_norm_g3 = a_norm_g[:, None, :]
    b_w_qkv_h, b_w_out_h = jnp.swapaxes(b_w_qkv, 1, 2).astype(BF16), b_w_out.astype(BF16)
    c_w_qkv_h, c_w_out_h = jnp.swapaxes(c_w_qkv, 1, 2).astype(BF16), c_w_out.astype(BF16)
    tables_b = _rope_tables(HD_B)
    tables_c = _rope_tables(HD_C)
    cache_b = (cache_b_k.reshape(DEC_BATCH, -1, PAST_LEN, KV_B * HD_B),
               cache_b_v.reshape(DEC_BATCH, -1, PAST_LEN, KV_B * HD_B))
    cache_c = (cache_c_k.reshape(DEC_BATCH, -1, PAST_LEN, KV_C * HD_C),
               cache_c_v.reshape(DEC_BATCH, -1, PAST_LEN, KV_C * HD_C))

    new_kv = {1: [], 2: []}
    for l in range(DEPTH):
        kind, idx = l % 3, l // 3
        x = _ffn(x, mods, norm_g, wg, wu, wd, l, 0)
        if kind == 0:
            x = _mixer_a(x, mods, norm_g, a_w_in_h, a_norm_g3, a_w_s_h, a_b_s_t, a_w_out_h, l, idx)
        elif kind == 1:
            q_g = jnp.broadcast_to(b_q_g[:, :, None], b_q_g.shape + (LANES,))
            k_g = jnp.broadcast_to(b_k_g[:, :, None], b_k_g.shape + (LANES,))
            qc, kc, vc, kf, vf = _qkv_b(x, mods, norm_g, b_w_qkv_h, q_g, k_g, None, None, l, idx, False)
            ql, kl, vl = _qkv_b(x, mods, norm_g, b_w_qkv_h, q_g, k_g, tables_b, cache_b, l, idx, True)
            new_kv[1].append((kf.reshape(BATCH, SEQ, KV_B, HD_B), vf.reshape(BATCH, SEQ, KV_B, HD_B)))
            x = _attn_b(x, qc, kc, vc, mods, b_w_out_h, l, idx, False)
            x = _attn_b(x, ql, kl, vl, mods, b_w_out_h, l, idx, True)
        else:
            qc, kc, vc, kf, vf = _qkv_c(x, mods, norm_g, c_w_qkv_h, None, None, l, idx, False)
            ql, kl, vl = _qkv_c(x, mods, norm_g, c_w_qkv_h, tables_c, cache_c, l, idx, True)
            new_kv[2].append((kf.reshape(BATCH, SEQ, KV_C, HD_C), vf.reshape(BATCH, SEQ, KV_C, HD_C)))
            sink = c_sink[idx]
            x = _attn_c(x, qc, kc, vc, sink, mods, c_w_out_h, l, idx, False)
            x = _attn_c(x, ql, kl, vl, sink, mods, c_w_out_h, l, idx, True)
        if l < DEPTH - 1:
            x = _ffn(x, mods, norm_g, wg, wu, wd, l, 1)
    l = DEPTH - 1
    y_prompt = _ffn(x, mods, norm_g, wg, wu, wd, l, 1, final_g=final_g, rows=(0, N_CTX))
    y_sample = _ffn(x, mods, norm_g, wg, wu, wd, l, 1, final_g=final_g, rows=(N_CTX, N_TOK))
    new_b_k = jnp.stack([kv[0] for kv in new_kv[1]], axis=1)
    new_b_v = jnp.stack([kv[1] for kv in new_kv[1]], axis=1)
    new_c_k = jnp.stack([kv[0] for kv in new_kv[2]], axis=1)
    new_c_v = jnp.stack([kv[1] for kv in new_kv[2]], axis=1)
    return (y_prompt.reshape(BATCH, SEQ, D_MODEL), y_sample.reshape(DEC_BATCH, DEC_SEQ, D_MODEL),
            new_b_k, new_b_v, new_c_k, new_c_v)
```

```python
import functools

import jax
import jax.numpy as jnp
from jax import lax
from jax.experimental import pallas as pl
from jax.experimental.pallas import tpu as pltpu

F32 = jnp.float32
BF16 = jnp.bfloat16

D_MODEL = 1024
BATCH = 16
SEQ = 256
DEPTH = 4
DEC_BATCH = 4
DEC_SEQ = 4096
PAST_LEN = 512
GRID_W = 64
N_MOD = 9
D_FF = 2816
CHUNK = 128
D_A = 2 * D_MODEL
G_A = 8
C_A = D_A // G_A
HQ_B, KV_B, HD_B = 8, 2, 128
HQ_C, KV_C, HD_C = 16, 2, 64
WINDOW = 128
ROPE_THETA = 10000.0
EPS = 1e-6
NEG_INF = -1e30
LOG2E = 1.4426950408889634

N_CTX = BATCH * SEQ
N_LAT = DEC_BATCH * DEC_SEQ
N_TOK = N_CTX + N_LAT
GROUP_TOK = 4096
MOD_ROWS = 8
LANES = 128

TM = 512
TQ_LAT = 128
ATTN_TK = 512
BAND = TQ_LAT + 2 * WINDOW
VMEM_LIMIT = 56 * 1024 * 1024


def _params(n_grid):
    return pltpu.CompilerParams(dimension_semantics=("arbitrary",) * n_grid,
                                vmem_limit_bytes=VMEM_LIMIT)


def _resident(block_shape, index_map):
    return pl.BlockSpec(block_shape, index_map, pipeline_mode=pl.Buffered(1))


def _dot(a, b):
    return jnp.dot(a, b, preferred_element_type=F32)


def _dot_nt(a, b):
    return lax.dot_general(a, b, (((1,), (1,)), ((), ())), preferred_element_type=F32)


def _dot_tn(a, b):
    return lax.dot_general(a, b, (((0,), (0,)), ((), ())), preferred_element_type=F32)


def _rms(x):
    return x * lax.rsqrt(jnp.mean(x * x, axis=-1, keepdims=True) + EPS)


def _norm_mod(x, g, shift, scale):
    return (_rms(x) * g) * (1.0 + scale) + shift


ADA_TN = 2304


def _ada_kernel(cond_ref, w_ref, b_ref, o_ref):
    s = jax.nn.silu(cond_ref[...]).astype(BF16)
    o_ref[...] = _dot(s, w_ref[...].astype(BF16)) + b_ref[...]


def _ada_params(cond, w_mod, b_mod):
    n = N_MOD * D_MODEL
    out = pl.pallas_call(
        _ada_kernel,
        grid=(DEPTH, n // ADA_TN),
        in_specs=[
            pl.BlockSpec((MOD_ROWS, D_MODEL), lambda l, j: (0, 0)),
            pl.BlockSpec((None, D_MODEL, ADA_TN), lambda l, j: (l, 0, j)),
            pl.BlockSpec((None, 1, ADA_TN), lambda l, j: (l, 0, j)),
        ],
        out_specs=pl.BlockSpec((None, MOD_ROWS, ADA_TN), lambda l, j: (l, 0, j)),
        out_shape=jax.ShapeDtypeStruct((DEPTH, MOD_ROWS, n), F32),
        compiler_params=_params(2),
        name="ada_params",
    )(cond, w_mod, b_mod.reshape(DEPTH, 1, n))
    return out.reshape(DEPTH, MOD_ROWS, N_MOD, D_MODEL)


W_STEPS = 16
CTX_TILES = N_CTX // TM
N_TILES = N_TOK // TM


def _tile(i):
    return jnp.maximum(i - W_STEPS, 0)


def _w_chunk(rows, cols, index):
    return pl.BlockSpec((None,) * len(index) + (rows // W_STEPS, cols),
                        lambda i: (*index, jnp.minimum(i, W_STEPS - 1), 0))


def _stage(i, src_ref, dst_ref):
    rows = src_ref.shape[0]
    dst_ref[pl.ds(pl.multiple_of(i * rows, rows), rows), :] = src_ref[...].astype(BF16)


def _ffn_kernel(*refs, k, gi, mode):
    n_x = 2 if mode == "first" else 1
    x_refs = refs[:n_x]
    mod_ref, g_ref, wg_ref, wu_ref, wd_ref = refs[n_x:n_x + 5]
    if mode == "last":
        fg_ref, o_ctx_ref, o_lat_ref, wg_s, wu_s, wd_s = refs[n_x + 5:]
    else:
        o_ref, wg_s, wu_s, wd_s = refs[n_x + 5:]
    i = pl.program_id(0)

    @pl.when(i < W_STEPS)
    def _():
        _stage(i, wg_ref, wg_s)
        _stage(i, wu_ref, wu_s)
        _stage(i, wd_ref, wd_s)

    @pl.when(i >= W_STEPS)
    def _():
        is_ctx = i - W_STEPS < CTX_TILES
        x = jnp.where(is_ctx, x_refs[0][...], x_refs[1][...]) if mode == "first" else x_refs[0][...]
        h = _norm_mod(x, g_ref[gi:gi + 1, :], mod_ref[k:k + 1, :], mod_ref[k + 1:k + 2, :]).astype(BF16)
        a = (jax.nn.silu(_dot(h, wg_s[...])) * _dot(h, wu_s[...])).astype(BF16)
        y = x + (0.5 * mod_ref[k + 2:k + 3, :]) * _dot(a, wd_s[...])
        if mode == "last":
            y = _rms(y) * fg_ref[...]

            @pl.when(is_ctx)
            def _():
                o_ctx_ref[...] = y

            @pl.when(jnp.logical_not(is_ctx))
            def _():
                o_lat_ref[...] = y
        else:
            o_ref[...] = y


def _ffn(xs, mods, norm_g, wg, wu, wd, l, half, mode="mid", final_g=None):
    k, gi = (0, 0) if half == 0 else (6, 2)
    per_group = GROUP_TOK // TM
    tile_spec = pl.BlockSpec((TM, D_MODEL), lambda i: (_tile(i), 0))
    ctx_spec = pl.BlockSpec((TM, D_MODEL), lambda i: (jnp.minimum(_tile(i), CTX_TILES - 1), 0))
    lat_spec = pl.BlockSpec((TM, D_MODEL), lambda i: (jnp.maximum(_tile(i) - CTX_TILES, 0), 0))
    in_specs = [ctx_spec, lat_spec] if mode == "first" else [tile_spec]
    in_specs += [
        pl.BlockSpec((None, None, N_MOD, D_MODEL), lambda i: (l, _tile(i) // per_group, 0, 0)),
        pl.BlockSpec((None, 3, D_MODEL), lambda i: (l, 0, 0)),
        _w_chunk(D_MODEL, D_FF, (l, half)),
        _w_chunk(D_MODEL, D_FF, (l, half)),
        _w_chunk(D_FF, D_MODEL, (l, half)),
    ]
    args = list(xs) if mode == "first" else [xs]
    args += [mods, norm_g, wg, wu, wd]
    if mode == "last":
        in_specs.append(pl.BlockSpec((1, D_MODEL), lambda i: (0, 0)))
        args.append(final_g.reshape(1, D_MODEL))
        out_specs = [ctx_spec, lat_spec]
        out_shape = [jax.ShapeDtypeStruct((N_CTX, D_MODEL), F32), jax.ShapeDtypeStruct((N_LAT, D_MODEL), F32)]
    else:
        out_specs = tile_spec
        out_shape = jax.ShapeDtypeStruct((N_TOK, D_MODEL), F32)
    return pl.pallas_call(
        functools.partial(_ffn_kernel, k=k, gi=gi, mode=mode),
        grid=(W_STEPS + N_TILES,),
        in_specs=in_specs,
        out_specs=out_specs,
        out_shape=out_shape,
        scratch_shapes=[pltpu.VMEM((D_MODEL, D_FF), BF16), pltpu.VMEM((D_MODEL, D_FF), BF16),
                        pltpu.VMEM((D_FF, D_MODEL), BF16)],
        compiler_params=_params(1),
        name=f"ffn_l{l}_h{half}",
    )(*args)


def _sgu_kernel(x_ref, mod_ref, g_ref, win_ref, ng_ref, ws_ref, bs_ref, wout_ref, o_ref,
                win_s, wout_s, gated_ref):
    i = pl.program_id(0)

    @pl.when(i < W_STEPS)
    def _():
        _stage(i, win_ref, win_s)
        _stage(i, wout_ref, wout_s)

    @pl.when(i >= W_STEPS)
    def _():
        x = x_ref[...]
        h = _norm_mod(x, g_ref[1:2, :], mod_ref[3:4, :], mod_ref[4:5, :]).astype(BF16)
        uv = _dot(h, win_s[...])
        v = (_rms(uv[:, D_A:]) * ng_ref[...]).astype(BF16)
        for c in range(TM // CHUNK):
            rows = slice(c * CHUNK, (c + 1) * CHUNK)
            for g in range(G_A):
                cols = slice(g * C_A, (g + 1) * C_A)
                mixed = _dot(ws_ref[g], v[rows, cols]) + bs_ref[:, g:g + 1]
                gated_ref[rows, cols] = (uv[rows, cols] * mixed).astype(BF16)
        o_ref[...] = x + mod_ref[5:6, :] * _dot(gated_ref[...], wout_s[...])


def _mixer_a(x, mods, norm_g, w_in, a_norm_g, w_s, b_s_t, w_out, l, idx):
    per_group = GROUP_TOK // TM
    tile_spec = pl.BlockSpec((TM, D_MODEL), lambda i: (_tile(i), 0))
    return pl.pallas_call(
        _sgu_kernel,
        grid=(W_STEPS + N_TILES,),
        in_specs=[
            tile_spec,
            pl.BlockSpec((None, None, N_MOD, D_MODEL), lambda i: (l, _tile(i) // per_group, 0, 0)),
            pl.BlockSpec((None, 3, D_MODEL), lambda i: (l, 0, 0)),
            _w_chunk(D_MODEL, 2 * D_A, (idx,)),
            pl.BlockSpec((None, 1, D_A), lambda i: (idx, 0, 0)),
            pl.BlockSpec((None, G_A, CHUNK, CHUNK), lambda i: (idx, 0, 0, 0)),
            pl.BlockSpec((None, CHUNK, G_A), lambda i: (idx, 0, 0)),
            _w_chunk(D_A, D_MODEL, (idx,)),
        ],
        out_specs=tile_spec,
        out_shape=jax.ShapeDtypeStruct((N_TOK, D_MODEL), F32),
        scratch_shapes=[pltpu.VMEM((D_MODEL, 2 * D_A), BF16), pltpu.VMEM((D_A, D_MODEL), BF16),
                        pltpu.VMEM((TM, D_A), BF16)],
        compiler_params=_params(1),
        name=f"mixer_a_l{l}",
    )(x, mods, norm_g, w_in, a_norm_g, w_s, b_s_t, w_out)


def _rope_t(y, cos, sin, quarter):
    pieces = []
    for s in range(0, y.shape[0], 2 * quarter):
        pieces += [y[s + quarter:s + 2 * quarter], y[s:s + quarter]]
    return y * cos + jnp.concatenate(pieces, axis=0) * sin


def _rope_tables(hd):
    rows = DEC_SEQ // GRID_W
    row = jnp.repeat(jnp.arange(rows), GRID_W).astype(F32)
    col = (jnp.arange(DEC_SEQ) % GRID_W).astype(F32)
    quarter = hd // 4
    inv = ROPE_THETA ** (-jnp.arange(quarter, dtype=F32) / quarter)
    ang_r = row[:, None] * inv[None, :]
    ang_c = col[:, None] * inv[None, :]
    cos = jnp.concatenate([jnp.cos(ang_r)] * 2 + [jnp.cos(ang_c)] * 2, axis=-1)
    sin = jnp.concatenate([-jnp.sin(ang_r), jnp.sin(ang_r), -jnp.sin(ang_c), jnp.sin(ang_c)], axis=-1)
    reps = LANES // hd
    return jnp.tile(cos, (1, reps)).T, jnp.tile(sin, (1, reps)).T


def _attention_pipeline(chains, order, s_buf, p_buf, acc_ref):
    n = len(order)
    state = [ch[2] for ch in chains]
    rows_of, max_of, alpha_of = [None] * n, [None] * n, [None] * n
    left = [len(ch[1]) for ch in chains]
    outs = [None] * len(chains)

    def scores(t):
        c, b = order[t]
        k_fn, _, mask = chains[c][1][b]
        st = _dot(k_fn(), chains[c][0]())
        if mask is not None:
            st = jnp.where(mask, st, NEG_INF)
        rows_of[t] = st.shape[0]
        s_buf[t % 2, :st.shape[0]] = st
        max_of[t] = jnp.max(st, axis=0, keepdims=True)

    def softmax(t):
        c, _ = order[t]
        m, l = state[c]
        st = s_buf[t % 2, :rows_of[t]]
        m_new = jnp.maximum(m, max_of[t])
        alpha_of[t] = jnp.exp2(m - m_new)
        p = jnp.exp2(st - m_new)
        p_buf[t % 2, :rows_of[t]] = p.astype(BF16)
        state[c] = (m_new, alpha_of[t] * l + jnp.sum(p, axis=0, keepdims=True))

    def weighted_values(t):
        c, b = order[t]
        pv = _dot(chains[c][1][b][1](), p_buf[t % 2, :rows_of[t]])
        acc_ref[c] = alpha_of[t] * acc_ref[c] + pv
        left[c] -= 1
        if left[c] == 0:
            outs[c] = acc_ref[c] / state[c][1]

    acc_ref[...] = jnp.zeros_like(acc_ref)
    for t in range(-1, n + 1):
        if t + 1 < n:
            scores(t + 1)
        if 0 <= t < n:
            softmax(t)
        if 0 <= t - 1:
            weighted_values(t - 1)
    return outs


def _attention_scratch(n_chains, n_max, nq, hd):
    return [pltpu.VMEM((2, n_max, nq), F32), pltpu.VMEM((2, n_max, nq), BF16),
            pltpu.VMEM((n_chains, hd, nq), F32)]


def _block_major(chains_blocks):
    depth = max(chains_blocks)
    return [(c, b) for b in range(depth) for c, nb in enumerate(chains_blocks) if b < nb]


QKV_B = (HQ_B + 2 * KV_B) * HD_B
SKV_B_LAT = PAST_LEN + DEC_SEQ


def _qkv_b_kernel(x_ref, mod_ref, g_ref, wt_ref, qg_ref, kg_ref, *rest, latent):
    if latent:
        cos_ref, sin_ref, kc_ref, vc_ref, q_ref, k_ref, v_ref = rest
    else:
        q_ref, k_ref, v_ref, kf_ref, vf_ref = rest

    def project():
        h = _norm_mod(x_ref[...], g_ref[1:2, :], mod_ref[3:4, :], mod_ref[4:5, :]).astype(BF16)
        qkv_t = _dot_nt(wt_ref[...], h)
        scale = HD_B ** -0.5 * LOG2E
        qg = jnp.tile(qg_ref[...], (1, TM // LANES))
        kg = jnp.tile(kg_ref[...], (1, TM // LANES))
        for j in range(HQ_B + KV_B):
            is_q = j < HQ_B
            y = qkv_t[j * HD_B:(j + 1) * HD_B, :]
            y = y * lax.rsqrt(jnp.mean(y * y, axis=0, keepdims=True) + EPS) * (qg if is_q else kg)
            kcols = slice((j - HQ_B) * HD_B, (j - HQ_B + 1) * HD_B)
            if not latent and not is_q:
                kf_ref[:, kcols] = y.T
            if latent:
                y = _rope_t(y, cos_ref[...], sin_ref[...], HD_B // 4)
            if is_q:
                q_ref[j] = (y * scale).astype(BF16)
            else:
                k_ref[:, kcols] = y.T.astype(BF16)
        v_t = qkv_t[(HQ_B + KV_B) * HD_B:, :]
        v_ref[...] = v_t.astype(BF16)
        if not latent:
            vf_ref[...] = v_t.T

    if latent:
        t = pl.program_id(1)

        @pl.when(t == 0)
        def _():
            k_ref[...] = kc_ref[...].astype(BF16)
            v_ref[...] = vc_ref[...].T.astype(BF16)

        pl.when(t > 0)(project)
    else:
        project()


def _qkv_specs(latent, l):
    per_batch = DEC_SEQ // TM
    if latent:
        assert PAST_LEN == TM
        grid = (DEC_BATCH, 1 + per_batch)

        def tile(b, t):
            return b * per_batch + jnp.maximum(t - 1, 0)

        x_map = lambda b, t: (N_CTX // TM + tile(b, t), 0)
        mod_map = lambda b, t: (l, 1 + b, 0, 0)
        q_map = lambda b, t: (0, 0, tile(b, t))
        kv_map = lambda b, t: (b * (1 + per_batch) + t, 0)
        vt_map = lambda b, t: (0, b * (1 + per_batch) + t)
        pos_map = lambda b, t: (0, jnp.maximum(t - 1, 0))
        return grid, N_LAT, DEC_BATCH * SKV_B_LAT, x_map, mod_map, q_map, kv_map, vt_map, pos_map
    x_map = lambda i: (i, 0)
    return ((N_CTX // TM,), N_CTX, N_CTX, x_map, lambda i: (l, 0, 0, 0), lambda i: (0, 0, i), x_map,
            lambda i: (0, i), None)


def _qkv_b(x, mods, norm_g, w_qkv, q_g, k_g, tables, cache, l, idx, latent):
    kvw = KV_B * HD_B
    grid, n, kv_rows, x_map, mod_map, q_map, kv_map, vt_map, pos_map = _qkv_specs(latent, l)
    const = lambda *_: (idx, 0, 0)
    in_specs = [
        pl.BlockSpec((TM, D_MODEL), x_map),
        pl.BlockSpec((None, None, N_MOD, D_MODEL), mod_map),
        pl.BlockSpec((None, 3, D_MODEL), lambda *_: (l, 0, 0)),
        _resident((None, QKV_B, D_MODEL), const),
        pl.BlockSpec((None, HD_B, LANES), const),
        pl.BlockSpec((None, HD_B, LANES), const),
    ]
    args = [x, mods, norm_g, w_qkv, q_g, k_g]
    out_specs = [
        pl.BlockSpec((HQ_B, HD_B, TM), q_map),
        pl.BlockSpec((TM, kvw), kv_map),
        pl.BlockSpec((kvw, TM), vt_map),
    ]
    out_shape = [jax.ShapeDtypeStruct((HQ_B, HD_B, n), BF16), jax.ShapeDtypeStruct((kv_rows, kvw), BF16),
                 jax.ShapeDtypeStruct((kvw, kv_rows), BF16)]
    if latent:
        in_specs += [pl.BlockSpec((LANES, TM), pos_map)] * 2
        in_specs += [pl.BlockSpec((None, None, PAST_LEN, kvw), lambda b, t: (b, idx, 0, 0))] * 2
        args += list(tables) + list(cache)
    else:
        out_specs += [pl.BlockSpec((TM, kvw), kv_map)] * 2
        out_shape += [jax.ShapeDtypeStruct((n, kvw), F32)] * 2
    return pl.pallas_call(
        functools.partial(_qkv_b_kernel, latent=latent),
        grid=grid,
        in_specs=in_specs,
        out_specs=out_specs,
        out_shape=out_shape,
        compiler_params=_params(len(grid)),
        name=f"qkv_b_{'lat' if latent else 'ctx'}",
    )(*args)


def _attn_b_kernel(q_ref, k_ref, v_ref, x_ref, mod_ref, wout_ref, o_ref, s_buf, p_buf, acc_ref, *, tq, tk):
    group = HQ_B // KV_B
    nq = group * tq
    nblk = k_ref.shape[0] // tk
    row = jnp.zeros((1, nq), F32)

    def chain(kh):
        cols = slice(kh * HD_B, (kh + 1) * HD_B)
        q_fn = lambda: jnp.concatenate([q_ref[kh * group + g] for g in range(group)], axis=1)
        blocks = [(lambda j=j: k_ref[j * tk:(j + 1) * tk, cols], lambda j=j: v_ref[cols, j * tk:(j + 1) * tk], None)
                  for j in range(nblk)]
        return q_fn, blocks, (row + NEG_INF, row)

    outs = _attention_pipeline([chain(kh) for kh in range(KV_B)], _block_major([nblk] * KV_B),
                               s_buf, p_buf, acc_ref)
    heads = [outs[kh][:, g * tq:(g + 1) * tq].T for kh in range(KV_B) for g in range(group)]
    attn = jnp.concatenate(heads, axis=1).astype(BF16)
    o_ref[...] = x_ref[...] + mod_ref[5:6, :] * _dot(attn, wout_ref[...])


def _attn_geometry(latent):
    if latent:
        return TQ_LAT, DEC_BATCH, DEC_SEQ // TQ_LAT, N_CTX // TQ_LAT
    return SEQ, BATCH, 1, 0


def _attn_b(x, q, k, v, mods, w_out, l, idx, latent):
    kvw = KV_B * HD_B
    tq, nb, nqb, x0 = _attn_geometry(latent)
    skv = SKV_B_LAT if latent else SEQ
    tk = min(ATTN_TK, skv)
    nq = (HQ_B // KV_B) * tq
    per_group = GROUP_TOK // tq
    x_spec = pl.BlockSpec((tq, D_MODEL), lambda b, i: (x0 + b * nqb + i, 0))
    in_specs = [
        pl.BlockSpec((HQ_B, HD_B, tq), lambda b, i: (0, 0, b * nqb + i)),
        pl.BlockSpec((skv, kvw), lambda b, i: (b, 0)),
        pl.BlockSpec((kvw, skv), lambda b, i: (0, b)),
        x_spec,
        pl.BlockSpec((None, None, N_MOD, D_MODEL), lambda b, i: (l, (x0 + b * nqb + i) // per_group, 0, 0)),
        _resident((None, HQ_B * HD_B, D_MODEL), lambda b, i: (idx, 0, 0)),
    ]
    return pl.pallas_call(
        functools.partial(_attn_b_kernel, tq=tq, tk=tk),
        grid=(nb, nqb),
        in_specs=in_specs,
        out_specs=x_spec,
        out_shape=jax.ShapeDtypeStruct((N_TOK, D_MODEL), F32),
        scratch_shapes=_attention_scratch(KV_B, tk, nq, HD_B),
        input_output_aliases={3: 0},
        compiler_params=_params(2),
        name=f"attn_b_{'lat' if latent else 'ctx'}",
    )(q, k, v, x, mods, w_out)


QKV_C = (HQ_C + 2 * KV_C) * HD_C
Q_SLABS_C = HQ_C * HD_C // LANES
KV_PAD_C = 4 * LANES
SKV_C_LAT = PAST_LEN + DEC_SEQ


def _pad_variants(y):
    lane = lax.broadcasted_iota(jnp.int32, y.shape, 1)
    low = lane < HD_C
    swapped = pltpu.roll(y, HD_C, 1)
    zero = jnp.zeros_like(y)
    return jnp.concatenate([jnp.where(low, y, zero), jnp.where(low, zero, swapped),
                            jnp.where(low, swapped, zero), jnp.where(low, zero, y)], axis=1)


def _pad_variants_t(y_t):
    h0, h1 = y_t[:HD_C], y_t[HD_C:]
    zero = jnp.zeros_like(h0)
    return jnp.concatenate([h0, zero, zero, h0, h1, zero, zero, h1], axis=0)


def _qkv_c_kernel(x_ref, mod_ref, g_ref, wt_ref, *rest, latent):
    if latent:
        cos_ref, sin_ref, kc_ref, vc_ref, q_ref, k_ref, v_ref = rest
    else:
        q_ref, k_ref, v_ref, kf_ref, vf_ref = rest

    def project():
        h = _norm_mod(x_ref[...], g_ref[1:2, :], mod_ref[3:4, :], mod_ref[4:5, :]).astype(BF16)
        qkv_t = _dot_nt(wt_ref[...], h)
        scale = HD_C ** -0.5 * LOG2E
        for j in range(Q_SLABS_C):
            y = qkv_t[j * LANES:(j + 1) * LANES, :]
            if latent:
                y = _rope_t(y, cos_ref[...], sin_ref[...], HD_C // 4)
            q_ref[j] = (y * scale).astype(BF16)
        k_t = qkv_t[D_MODEL:D_MODEL + LANES, :]
        v_t = qkv_t[D_MODEL + LANES:, :]
        if latent:
            k_t = _rope_t(k_t, cos_ref[...], sin_ref[...], HD_C // 4)
        k = k_t.T
        if not latent:
            kf_ref[...] = k
            vf_ref[...] = v_t.T
        k_ref[...] = _pad_variants(k).astype(BF16)
        v_ref[...] = _pad_variants_t(v_t).astype(BF16)

    if latent:
        t = pl.program_id(1)

        @pl.when(t == 0)
        def _():
            k_ref[...] = _pad_variants(kc_ref[...]).astype(BF16)
            v_ref[...] = _pad_variants_t(vc_ref[...].T).astype(BF16)

        pl.when(t > 0)(project)
    else:
        project()


def _qkv_c(x, mods, norm_g, w_qkv, tables, cache, l, idx, latent):
    grid, n, kv_rows, x_map, mod_map, q_map, kv_map, vt_map, pos_map = _qkv_specs(latent, l)
    in_specs = [
        pl.BlockSpec((TM, D_MODEL), x_map),
        pl.BlockSpec((None, None, N_MOD, D_MODEL), mod_map),
        pl.BlockSpec((None, 3, D_MODEL), lambda *_: (l, 0, 0)),
        _resident((None, QKV_C, D_MODEL), lambda *_: (idx, 0, 0)),
    ]
    args = [x, mods, norm_g, w_qkv]
    out_specs = [
        pl.BlockSpec((Q_SLABS_C, LANES, TM), q_map),
        pl.BlockSpec((TM, KV_PAD_C), kv_map),
        pl.BlockSpec((KV_PAD_C, TM), vt_map),
    ]
    out_shape = [jax.ShapeDtypeStruct((Q_SLABS_C, LANES, n), BF16), jax.ShapeDtypeStruct((kv_rows, KV_PAD_C), BF16),
                 jax.ShapeDtypeStruct((KV_PAD_C, kv_rows), BF16)]
    if latent:
        in_specs += [pl.BlockSpec((LANES, TM), pos_map)] * 2
        in_specs += [pl.BlockSpec((None, None, PAST_LEN, LANES), lambda b, t: (b, idx, 0, 0))] * 2
        args += list(tables) + list(cache)
    else:
        out_specs += [pl.BlockSpec((TM, LANES), kv_map)] * 2
        out_shape += [jax.ShapeDtypeStruct((n, LANES), F32)] * 2
    return pl.pallas_call(
        functools.partial(_qkv_c_kernel, latent=latent),
        grid=grid,
        in_specs=in_specs,
        out_specs=out_specs,
        out_shape=out_shape,
        compiler_params=_params(len(grid)),
        name=f"qkv_c_{'lat' if latent else 'ctx'}",
    )(*args)


def _attn_c_kernel(sink_ref, q_ref, k_ref, v_ref, x_ref, mod_ref, wout_ref, o_ref, s_buf, p_buf, acc_ref,
                   *, tq, latent):
    group = HQ_C // KV_C
    pairs = group // 2
    nq = pairs * tq
    if latent:
        i = pl.program_id(1)
        start = pl.multiple_of(jnp.clip(i * tq - WINDOW, 0, DEC_SEQ - BAND), WINDOW)
        band = pl.ds(pl.multiple_of(PAST_LEN + start, WINDOW), BAND)
        qpos = i * tq + (lax.broadcasted_iota(jnp.int32, (BAND, nq), 1) & (tq - 1))
        kpos = start + lax.broadcasted_iota(jnp.int32, (BAND, nq), 0)
        near = jnp.abs(qpos - kpos) <= WINDOW
        spans = [(slice(0, PAST_LEN), None), (band, near)]
    else:
        spans = [(slice(None), None)]

    def chain(kh, par):
        cols = slice((2 * kh + par) * LANES, (2 * kh + par + 1) * LANES)
        q_fn = lambda: jnp.concatenate([q_ref[kh * pairs + j] for j in range(pairs)], axis=1)
        blocks = [(lambda r=r: k_ref[r, cols], lambda r=r: v_ref[cols, r], mask) for r, mask in spans]
        sink = jnp.concatenate([jnp.full((1, tq), sink_ref[kh * group + 2 * j + par] * LOG2E, F32)
                                for j in range(pairs)], axis=1)
        return q_fn, blocks, (sink, jnp.ones_like(sink))

    chains = [chain(kh, par) for kh in range(KV_C) for par in range(2)]
    outs = _attention_pipeline(chains, _block_major([len(spans)] * len(chains)), s_buf, p_buf, acc_ref)
    slabs = []
    for kh in range(KV_C):
        o_t = outs[2 * kh] + outs[2 * kh + 1]
        slabs += [o_t[:, j * tq:(j + 1) * tq].T for j in range(pairs)]
    attn = jnp.concatenate(slabs, axis=1).astype(BF16)
    o_ref[...] = x_ref[...] + mod_ref[5:6, :] * _dot(attn, wout_ref[...])


def _attn_c(x, q, k, v, sink, mods, w_out, l, idx, latent):
    tq, nb, nqb, x0 = _attn_geometry(latent)
    skv = SKV_C_LAT if latent else SEQ
    nq = (HQ_C // KV_C // 2) * tq
    per_group = GROUP_TOK // tq
    x_spec = pl.BlockSpec((tq, D_MODEL), lambda b, i: (x0 + b * nqb + i, 0))
    in_specs = [
        pl.BlockSpec(memory_space=pltpu.SMEM),
        pl.BlockSpec((Q_SLABS_C, LANES, tq), lambda b, i: (0, 0, b * nqb + i)),
        pl.BlockSpec((skv, KV_PAD_C), lambda b, i: (b, 0)),
        pl.BlockSpec((KV_PAD_C, skv), lambda b, i: (0, b)),
        x_spec,
        pl.BlockSpec((None, None, N_MOD, D_MODEL), lambda b, i: (l, (x0 + b * nqb + i) // per_group, 0, 0)),
        _resident((None, HQ_C * HD_C, D_MODEL), lambda b, i: (idx, 0, 0)),
    ]
    return pl.pallas_call(
        functools.partial(_attn_c_kernel, tq=tq, latent=latent),
        grid=(nb, nqb),
        in_specs=in_specs,
        out_specs=x_spec,
        out_shape=jax.ShapeDtypeStruct((N_TOK, D_MODEL), F32),
        scratch_shapes=_attention_scratch(2 * KV_C, PAST_LEN if latent else SEQ, nq, LANES),
        input_output_aliases={4: 0},
        compiler_params=_params(2),
        name=f"attn_c_{'lat' if latent else 'ctx'}",
    )(sink, q, k, v, x, mods, w_out)


def kernel(x_prompt, x_sample, cache_b_k, cache_b_v, cache_c_k, cache_c_v, c, c_ctx, w_mod, b_mod, norm_g,
           ffn_w_gate, ffn_w_up, ffn_w_down, a_w_in, a_norm_g, a_w_s, a_b_s, a_w_out, b_w_qkv, b_q_g, b_k_g,
           b_w_out, c_w_qkv, c_sink, c_w_out, final_g):
    cond = jnp.concatenate([c_ctx[None, :], c, jnp.zeros((MOD_ROWS - 1 - DEC_BATCH, D_MODEL), F32)], axis=0)
    mods = _ada_params(cond, w_mod, b_mod)

    wg, wu, wd = ffn_w_gate, ffn_w_up, ffn_w_down
    a_w_s_h = a_w_s.astype(BF16)
    a_b_s_t = jnp.swapaxes(a_b_s, 1, 2)
    a_norm_g3 = a_norm_g[:, None, :]
    b_w_qkv_h, b_w_out_h = jnp.swapaxes(b_w_qkv, 1, 2).astype(BF16), b_w_out.astype(BF16)
    c_w_qkv_h, c_w_out_h = jnp.swapaxes(c_w_qkv, 1, 2).astype(BF16), c_w_out.astype(BF16)
    tables_b = _rope_tables(HD_B)
    tables_c = _rope_tables(HD_C)
    cache_b = (cache_b_k.reshape(DEC_BATCH, -1, PAST_LEN, KV_B * HD_B),
               cache_b_v.reshape(DEC_BATCH, -1, PAST_LEN, KV_B * HD_B))
    cache_c = (cache_c_k.reshape(DEC_BATCH, -1, PAST_LEN, KV_C * HD_C),
               cache_c_v.reshape(DEC_BATCH, -1, PAST_LEN, KV_C * HD_C))

    new_kv = {1: [], 2: []}
    for l in range(DEPTH):
        kind, idx = l % 3, l // 3
        if l == 0:
            x = _ffn((x_prompt.reshape(N_CTX, D_MODEL), x_sample.reshape(N_LAT, D_MODEL)), mods, norm_g,
                     wg, wu, wd, l, 0, mode="first")
        else:
            x = _ffn(x, mods, norm_g, wg, wu, wd, l, 0)
        if kind == 0:
            x = _mixer_a(x, mods, norm_g, a_w_in, a_norm_g3, a_w_s_h, a_b_s_t, a_w_out, l, idx)
        elif kind == 1:
            q_g = jnp.broadcast_to(b_q_g[:, :, None], b_q_g.shape + (LANES,))
            k_g = jnp.broadcast_to(b_k_g[:, :, None], b_k_g.shape + (LANES,))
            qc, kc, vc, kf, vf = _qkv_b(x, mods, norm_g, b_w_qkv_h, q_g, k_g, None, None, l, idx, False)
            ql, kl, vl = _qkv_b(x, mods, norm_g, b_w_qkv_h, q_g, k_g, tables_b, cache_b, l, idx, True)
            new_kv[1].append((kf.reshape(BATCH, SEQ, KV_B, HD_B), vf.reshape(BATCH, SEQ, KV_B, HD_B)))
            x = _attn_b(x, qc, kc, vc, mods, b_w_out_h, l, idx, False)
            x = _attn_b(x, ql, kl, vl, mods, b_w_out_h, l, idx, True)
        else:
            qc, kc, vc, kf, vf = _qkv_c(x, mods, norm_g, c_w_qkv_h, None, None, l, idx, False)
            ql, kl, vl = _qkv_c(x, mods, norm_g, c_w_qkv_h, tables_c, cache_c, l, idx, True)
            new_kv[2].append((kf.reshape(BATCH, SEQ, KV_C, HD_C), vf.reshape(BATCH, SEQ, KV_C, HD_C)))
            sink = c_sink[idx]
            x = _attn_c(x, qc, kc, vc, sink, mods, c_w_out_h, l, idx, False)
            x = _attn_c(x, ql, kl, vl, sink, mods, c_w_out_h, l, idx, True)
        if l < DEPTH - 1:
            x = _ffn(x, mods, norm_g, wg, wu, wd, l, 1)
    y_prompt, y_sample = _ffn(x, mods, norm_g, wg, wu, wd, DEPTH - 1, 1, mode="last", final_g=final_g)
    new_b_k = jnp.stack([kv[0] for kv in new_kv[1]], axis=1)
    new_b_v = jnp.stack([kv[1] for kv in new_kv[1]], axis=1)
    new_c_k = jnp.stack([kv[0] for kv in new_kv[2]], axis=1)
    new_c_v = jnp.stack([kv[1] for kv in new_kv[2]], axis=1)
    return (y_prompt.reshape(BATCH, SEQ, D_MODEL), y_sample.reshape(DEC_BATCH, DEC_SEQ, D_MODEL),
            new_b_k, new_b_v, new_c_k, new_c_v)
```
